```python
import jax, jax.numpy as jnp
from jax import lax
import numpy as np

D_MODEL = 1024
BATCH = 2
SEQ = 16384
DEPTH = 2

CTX_LEN = 256
GRID_W = 64
N_MIXERS = 2
N_A = (DEPTH + 1) // 2
N_B = DEPTH // 2
D_RNN = 1280
RG_HEADS = 16
RG_HEAD_DIM = D_RNN // RG_HEADS
RG_CONV_W = 4
RG_CONV_PAD = (2, 1)
RG_C = 8.0
CF_CONV_W = 31
CF_CONV_PAD = (15, 15)
D_FF = 2816
FFN_CONV_W = 3
EPS = 1e-6

kernel_name = "hybrid_rglru_conformer_convffn_prefix_ctx"


def rmsnorm(x, g):
    xf = x.astype(jnp.float32)
    y = xf * lax.rsqrt(jnp.mean(xf * xf, axis=-1, keepdims=True) + EPS)
    return (y * g.astype(jnp.float32)).astype(x.dtype)


def layernorm(x, g, b):
    xf = x.astype(jnp.float32)
    mu = jnp.mean(xf, axis=-1, keepdims=True)
    var = jnp.mean(jnp.square(xf - mu), axis=-1, keepdims=True)
    y = (xf - mu) * lax.rsqrt(var + EPS)
    return (y * g.astype(jnp.float32) + b.astype(jnp.float32)).astype(x.dtype)


def dwconv1d(u, w, b, pad):
    y = lax.conv_general_dilated(u, w[:, None, :].astype(u.dtype), window_strides=(1,), padding=[pad],
                                 dimension_numbers=('NWC', 'WIO', 'NWC'), feature_group_count=u.shape[-1])
    return y + b


def dwconv2d(u, w, b, rows, width):
    bsz, seq, ch = u.shape
    g = u.reshape(bsz, rows, width, ch)
    y = lax.conv_general_dilated(g, w[:, :, None, :].astype(u.dtype), window_strides=(1, 1),
                                 padding=((1, 1), (1, 1)), dimension_numbers=('NHWC', 'HWIO', 'NHWC'),
                                 feature_group_count=ch)
    return y.reshape(bsz, seq, ch) + b


def _scan_combine(left, right):
    a_l, b_l = left
    a_r, b_r = right
    return a_l * a_r, a_r * b_l + b_r


def linear_scan(log_a, b, h0, reverse):
    A, H = lax.associative_scan(_scan_combine, (jnp.exp(log_a), b), axis=1, reverse=reverse)
    return H + A * h0[:, None, :]


def rglru_core(xm, h0_f, h0_b, w_in, conv_w, conv_b, wa, ba, wx, bx, lam):
    u = xm @ w_in
    ug, ux = jnp.split(u, 2, axis=-1)
    ux = dwconv1d(ux, conv_w, conv_b, RG_CONV_PAD).astype(jnp.float32)
    bsz, seq, _ = ux.shape
    uh = ux.reshape(bsz, seq, RG_HEADS, RG_HEAD_DIM)
    r = jax.nn.sigmoid(jnp.einsum('bshd,zhde->zbshe', uh, wa.astype(jnp.float32))
                       + ba.astype(jnp.float32)[:, None, None]).reshape(2, bsz, seq, D_RNN)
    ig = jax.nn.sigmoid(jnp.einsum('bshd,zhde->zbshe', uh, wx.astype(jnp.float32))
                        + bx.astype(jnp.float32)[:, None, None]).reshape(2, bsz, seq, D_RNN)
    log_a = -RG_C * r * jax.nn.softplus(-lam.astype(jnp.float32))[:, None, None, :]
    b = jnp.sqrt(-jnp.expm1(2.0 * log_a)) * (ig * ux[None])
    hf = linear_scan(log_a[0], b[0], h0_f, reverse=False)
    hb = linear_scan(log_a[1], b[1], h0_b, reverse=True)
    return hf, hb, ug


def rglru_out(y, ug, w_out):
    v = (y * jax.nn.gelu(ug.astype(jnp.float32))).astype(ug.dtype)
    return v @ w_out


def conformer_conv(xm, w1, b1, cw, cb, lng, lnb, w2, b2):
    u = xm @ w1 + b1
    a, g = jnp.split(u, 2, axis=-1)
    v = a * jax.nn.sigmoid(g)
    v = dwconv1d(v, cw, cb, CF_CONV_PAD)
    v = jax.nn.silu(layernorm(v, lng, lnb))
    return v @ w2 + b2


def conv_ffn(xm, w_up, cw, cb, w_down, rows, width):
    u = dwconv2d(xm @ w_up, cw, cb, rows, width)
    v, g = jnp.split(u, 2, axis=-1)
    return (jax.nn.silu(g) * v) @ w_down


def setup_inputs(seed: int = 0) -> dict:
    key = jax.random.key(seed)
    ks = jax.random.split(key, 32)

    def nrm(k, shape, scale):
        return jax.random.normal(k, shape, jnp.float32) * scale

    D = D_MODEL
    a_init = jax.random.uniform(ks[14], (N_A, 2, D_RNN), jnp.float32, 0.9, 0.999)
    return {
        "x": nrm(ks[0], (BATCH, SEQ, D), 1.0),
        "c": nrm(ks[1], (BATCH, D), 1.0),
        "ctx": nrm(ks[2], (BATCH, CTX_LEN, D), 1.0),
        "c_ctx": nrm(ks[3], (D,), 1.0),
        "ada_w": nrm(ks[4], (DEPTH, D, 6 * D), 0.02),
        "ada_b": nrm(ks[5], (DEPTH, 6 * D), 0.02),
        "norm_mix": 1.0 + nrm(ks[6], (DEPTH, D), 0.05),
        "norm_ffn": 1.0 + nrm(ks[7], (DEPTH, D), 0.05),
        "rg_w_in": nrm(ks[8], (N_A, D, 2 * D_RNN), D ** -0.5),
        "rg_conv_w": nrm(ks[9], (N_A, RG_CONV_W, D_RNN), RG_CONV_W ** -0.5),
        "rg_conv_b": nrm(ks[10], (N_A, D_RNN), 0.02),
        "rg_wa": nrm(ks[11], (N_A, 2, RG_HEADS, RG_HEAD_DIM, RG_HEAD_DIM), RG_HEAD_DIM ** -0.5),
        "rg_ba": nrm(ks[12], (N_A, 2, RG_HEADS, RG_HEAD_DIM), 0.02),
        "rg_wx": nrm(ks[13], (N_A, 2, RG_HEADS, RG_HEAD_DIM, RG_HEAD_DIM), RG_HEAD_DIM ** -0.5),
        "rg_bx": nrm(ks[15], (N_A, 2, RG_HEADS, RG_HEAD_DIM), 0.02),
        "rg_lam": jnp.log(a_init) - jnp.log1p(-a_init),
        "rg_w_out": nrm(ks[16], (N_A, D_RNN, D), D_RNN ** -0.5),
        "cf_w_pw1": nrm(ks[17], (N_B, D, 2 * D), D ** -0.5),
        "cf_b_pw1": nrm(ks[18], (N_B, 2 * D), 0.02),
        "cf_conv_w": nrm(ks[19], (N_B, CF_CONV_W, D), CF_CONV_W ** -0.5),
        "cf_conv_b": nrm(ks[20], (N_B, D), 0.02),
        "cf_ln_g": 1.0 + nrm(ks[21], (N_B, D), 0.05),
        "cf_ln_b": nrm(ks[22], (N_B, D), 0.02),
        "cf_w_pw2": nrm(ks[23], (N_B, D, D), D ** -0.5),
        "cf_b_pw2": nrm(ks[24], (N_B, D), 0.02),
        "ffn_w_up": nrm(ks[25], (DEPTH, D, 2 * D_FF), D ** -0.5),
        "ffn_conv_w": nrm(ks[26], (DEPTH, FFN_CONV_W, FFN_CONV_W, 2 * D_FF), 1.0 / FFN_CONV_W),
        "ffn_conv_b": nrm(ks[27], (DEPTH, 2 * D_FF), 0.02),
        "ffn_w_down": nrm(ks[28], (DEPTH, D_FF, D), D_FF ** -0.5),
        "norm_final": 1.0 + nrm(ks[29], (D,), 0.05),
    }


def reference(x, c, ctx, c_ctx, ada_w, ada_b, norm_mix, norm_ffn,
              rg_w_in, rg_conv_w, rg_conv_b, rg_wa, rg_ba, rg_wx, rg_bx, rg_lam, rg_w_out,
              cf_w_pw1, cf_b_pw1, cf_conv_w, cf_conv_b, cf_ln_g, cf_ln_b, cf_w_pw2, cf_b_pw2,
              ffn_w_up, ffn_conv_w, ffn_conv_b, ffn_w_down, norm_final):
    bsz, seq, _ = x.shape
    ROWS = seq // GRID_W
    ctx_len = ctx.shape[1]
    h, hc = x, ctx
    s_c = jax.nn.silu(c)
    s_cc = jax.nn.silu(c_ctx)
    for i in range(DEPTH):
        ctx_live = any(j % N_MIXERS == 0 for j in range(i + 1, DEPTH))
        mod = s_c @ ada_w[i] + ada_b[i]
        sh1, sc1, g1, sh2, sc2, g2 = [t[:, None, :] for t in jnp.split(mod, 6, axis=-1)]
        modc = s_cc @ ada_w[i] + ada_b[i]
        csh1, csc1, cg1, csh2, csc2, cg2 = jnp.split(modc, 6, axis=-1)
        xm = rmsnorm(h, norm_mix[i]) * (1.0 + sc1) + sh1
        cm = rmsnorm(hc, norm_mix[i]) * (1.0 + csc1) + csh1
        if i % N_MIXERS == 0:
            k = i // N_MIXERS
            prm = (rg_w_in[k], rg_conv_w[k], rg_conv_b[k], rg_wa[k], rg_ba[k], rg_wx[k], rg_bx[k], rg_lam[k])
            zero = jnp.zeros((bsz, D_RNN), jnp.float32)
            hf_c, hb_c, ug_c = rglru_core(cm, zero, zero, *prm)
            hf, hb, ug = rglru_core(xm, hf_c[:, -1], hb_c[:, 0], *prm)
            h = h + g1 * rglru_out(hf + hb, ug, rg_w_out[k])
            if ctx_live:
                hc = hc + cg1 * rglru_out(hf_c + hb_c, ug_c, rg_w_out[k])
        else:
            k = i // N_MIXERS
            prm = (cf_w_pw1[k], cf_b_pw1[k], cf_conv_w[k], cf_conv_b[k], cf_ln_g[k], cf_ln_b[k],
                   cf_w_pw2[k], cf_b_pw2[k])
            h = h + g1 * conformer_conv(xm, *prm)
            if ctx_live:
                hc = hc + cg1 * conformer_conv(cm, *prm)
        xm2 = rmsnorm(h, norm_ffn[i]) * (1.0 + sc2) + sh2
        h = h + g2 * conv_ffn(xm2, ffn_w_up[i], ffn_conv_w[i], ffn_conv_b[i], ffn_w_down[i], ROWS, GRID_W)
        if ctx_live:
            cm2 = rmsnorm(hc, norm_ffn[i]) * (1.0 + csc2) + csh2
            hc = hc + cg2 * conv_ffn(cm2, ffn_w_up[i], ffn_conv_w[i], ffn_conv_b[i], ffn_w_down[i], 1, ctx_len)
    return rmsnorm(h, norm_final)
```

```python
import functools

import jax
import jax.numpy as jnp
from jax import lax
from jax.experimental import pallas as pl
from jax.experimental.pallas import tpu as pltpu

F32 = jnp.float32
BF16 = jnp.bfloat16

EPS = 1e-6
RG_C = 8.0
GRID_W = 64
RG_CONV_LEFT = 2
CF_CONV_LEFT = 15

SUBLANES = 8
LANES = 128
MXU_DIM = 256
VMEM_LIMIT_BYTES = 56 * 1024 * 1024

RG_HALO = SUBLANES
CF_HALO = 2 * SUBLANES
GATE_K = 2 * MXU_DIM


def _const_spec(shape):
    nd = len(shape)
    return pl.BlockSpec(shape, lambda *_: (0,) * nd, pipeline_mode=pl.Buffered(1))


def _rms_mod(x, g, scale, shift):
    ms = jnp.mean(x * x, axis=-1, keepdims=True)
    return (x * lax.rsqrt(ms + EPS) * g) * (1.0 + scale) + shift


def _mod_kernel(cc_ref, w_ref, b_ref, o_ref):
    s = cc_ref[...]
    s = s * jax.nn.sigmoid(s)
    o_ref[0] = jnp.dot(s, w_ref[0], preferred_element_type=F32) + b_ref[0]


def _modulation(cc, ada_w, ada_b):
    depth, d, n = ada_w.shape
    rows = cc.shape[0]
    nc = n // 6
    return pl.pallas_call(
        _mod_kernel,
        grid=(depth, n // nc),
        in_specs=[
            pl.BlockSpec((rows, d), lambda i, j: (0, 0)),
            pl.BlockSpec((1, d, nc), lambda i, j: (i, 0, j)),
            pl.BlockSpec((1, 1, nc), lambda i, j: (i, 0, j)),
        ],
        out_specs=pl.BlockSpec((1, rows, nc), lambda i, j: (i, 0, j)),
        out_shape=jax.ShapeDtypeStruct((depth, rows, n), F32),
        compiler_params=pltpu.CompilerParams(
            dimension_semantics=("arbitrary", "arbitrary"), vmem_limit_bytes=VMEM_LIMIT_BYTES),
        name="adaln_mod",
    )(cc, ada_w, ada_b.reshape(depth, 1, n))


def _gate_windows(d_rnn, head_dim):
    starts = []
    for j in range(d_rnn // MXU_DIM):
        first_row = (j * MXU_DIM // head_dim) * head_dim
        last_row = ((j * MXU_DIM + MXU_DIM - 1) // head_dim + 1) * head_dim
        k0 = min(first_row // LANES * LANES, d_rnn - GATE_K)
        assert k0 <= first_row and last_row <= k0 + GATE_K
        starts.append(k0)
    return tuple(starts)


def _pack_gate_weights(wa, wx, starts):
    dense_a = jax.scipy.linalg.block_diag(*[wa[h] for h in range(wa.shape[0])])
    dense_x = jax.scipy.linalg.block_diag(*[wx[h] for h in range(wx.shape[0])])
    groups = []
    for j, k0 in enumerate(starts):
        cols = slice(j * MXU_DIM, (j + 1) * MXU_DIM)
        groups.append(jnp.concatenate([dense_a[k0:k0 + GATE_K, cols], dense_x[k0:k0 + GATE_K, cols]], axis=1))
    return jnp.stack(groups).astype(BF16)


def _rglru_kernel(*refs, tile, n_tiles, reverse, mode, halo, gate_starts):
    refs = list(refs)
    if halo:
        xp_ref, xc_ref, xn_ref = refs[:3]
        refs = refs[3:]
    else:
        xc_ref = refs[0]
        refs = refs[1:]
    (mod_ref, ng_ref, winx_ref, cw_ref, cb_ref, wg_ref, ba_ref, bx_ref, lam_ref, h0_ref) = refs[:10]
    refs = refs[10:]
    if mode == "out":
        hb_ref, wing_ref, wout_ref = refs[:3]
        refs = refs[3:]
    out_ref, xm_buf, ux_buf, uh_buf, a_buf, b_buf, carry_ref = refs

    step = pl.program_id(1)
    ti = (n_tiles - 1 - step) if reverse else step
    shift, scale, gate = mod_ref[0, 0:1, :], mod_ref[0, 1:2, :], mod_ref[0, 2:3, :]
    ng = ng_ref[...]
    H = RG_HALO
    d_rnn = ux_buf.shape[1]

    xm_buf[H:H + tile, :] = _rms_mod(xc_ref[0], ng, scale, shift)
    zero_halo = jnp.zeros((H, xm_buf.shape[1]), F32)
    if halo:
        xm_buf[0:H, :] = jnp.where(ti > 0, _rms_mod(xp_ref[0], ng, scale, shift), zero_halo)
        xm_buf[H + tile:, :] = jnp.where(ti < n_tiles - 1, _rms_mod(xn_ref[0], ng, scale, shift), zero_halo)
    else:
        xm_buf[0:H, :] = zero_halo
        xm_buf[H + tile:, :] = zero_halo

    ux_buf[...] = jnp.dot(xm_buf[...].astype(BF16), winx_ref[...], preferred_element_type=F32)

    uh = cb_ref[...]
    for k in range(cw_ref.shape[0]):
        off = H - RG_CONV_LEFT + k
        uh = uh + cw_ref[k:k + 1, :] * ux_buf[off:off + tile, :]
    uh_buf[...] = uh

    softplus_neg_lam = jax.nn.softplus(-lam_ref[...])
    for j, k0 in enumerate(gate_starts):
        cols = slice(j * MXU_DIM, (j + 1) * MXU_DIM)
        pre = jnp.dot(uh_buf[:, k0:k0 + GATE_K].astype(BF16), wg_ref[j], preferred_element_type=F32)
        r = jax.nn.sigmoid(pre[:, :MXU_DIM] + ba_ref[:, cols])
        ig = jax.nn.sigmoid(pre[:, MXU_DIM:] + bx_ref[:, cols])
        log_a = (-RG_C) * r * softplus_neg_lam[:, cols]
        a = jnp.exp(log_a)
        a_buf[:, cols] = a
        b_buf[:, cols] = jnp.sqrt(jnp.tanh(-log_a) * (1.0 + a * a)) * (ig * uh_buf[:, cols])

    @pl.when(step == 0)
    def _():
        carry_ref[...] = jnp.broadcast_to(h0_ref[0], carry_ref.shape)

    row = lax.broadcasted_iota(jnp.int32, (SUBLANES, d_rnn), 0)
    n_blocks = tile // SUBLANES

    def scan_block(q, carry):
        blk = (n_blocks - 1 - q) if reverse else q
        r0 = pl.multiple_of(blk * SUBLANES, SUBLANES)
        a = a_buf[pl.ds(r0, SUBLANES), :]
        b = b_buf[pl.ds(r0, SUBLANES), :]
        for d in (1, 2, 4):
            if reverse:
                keep = row < SUBLANES - d
                sh = SUBLANES - d
            else:
                keep = row >= d
                sh = d
            a_s = jnp.where(keep, pltpu.roll(a, sh, 0), 1.0)
            b_s = jnp.where(keep, pltpu.roll(b, sh, 0), 0.0)
            b = a * b_s + b
            a = a * a_s
        h = b + a * carry
        b_buf[pl.ds(r0, SUBLANES), :] = h
        edge = h[0:1, :] if reverse else h[SUBLANES - 1:SUBLANES, :]
        return jnp.broadcast_to(edge, (SUBLANES, d_rnn))

    carry = lax.fori_loop(0, n_blocks, scan_block, carry_ref[...])
    carry_ref[...] = carry

    if mode == "state":
        out_ref[0] = carry[0:1, :]
    elif mode == "store":
        out_ref[0] = b_buf[...]
    else:
        ug = jnp.dot(xm_buf[H:H + tile, :].astype(BF16), wing_ref[...], preferred_element_type=F32)
        v = ((b_buf[...] + hb_ref[0]) * jax.nn.gelu(ug)).astype(BF16)
        y = jnp.dot(v, wout_ref[...], preferred_element_type=F32)
        out_ref[0] = xc_ref[0] + gate * y


def _rglru_call(x, modrows, ng, winx, cw, cb, wg, ba, bx, lam, h0, *, tile, reverse, mode, gate_starts,
                hb=None, wing=None, wout=None):
    bsz, seq, d = x.shape
    d_rnn = winx.shape[1]
    n_tiles = seq // tile
    halo = n_tiles > 1
    hpt = tile // RG_HALO
    n_hblk = seq // RG_HALO

    def t_of(i):
        return (n_tiles - 1 - i) if reverse else i

    in_specs, args = [], []
    if halo:
        in_specs.append(pl.BlockSpec((1, RG_HALO, d), lambda b, i: (b, jnp.maximum(t_of(i) * hpt - 1, 0), 0)))
        args.append(x)
    in_specs.append(pl.BlockSpec((1, tile, d), lambda b, i: (b, t_of(i), 0)))
    args.append(x)
    if halo:
        in_specs.append(
            pl.BlockSpec((1, RG_HALO, d), lambda b, i: (b, jnp.minimum((t_of(i) + 1) * hpt, n_hblk - 1), 0)))
        args.append(x)
    in_specs.append(pl.BlockSpec((1,) + modrows.shape[1:], lambda b, i: (b, 0, 0)))
    args.append(modrows)
    for w in (ng, winx, cw, cb, wg, ba, bx, lam):
        in_specs.append(_const_spec(w.shape))
        args.append(w)
    in_specs.append(pl.BlockSpec((1, 1, d_rnn), lambda b, i: (b, 0, 0)))
    args.append(h0)
    if mode == "out":
        in_specs.append(pl.BlockSpec((1, tile, d_rnn), lambda b, i: (b, t_of(i), 0)))
        args.append(hb)
        for w in (wing, wout):
            in_specs.append(_const_spec(w.shape))
            args.append(w)

    if mode == "state":
        out_spec = pl.BlockSpec((1, 1, d_rnn), lambda b, i: (b, 0, 0))
        out_shape = jax.ShapeDtypeStruct((bsz, 1, d_rnn), F32)
    elif mode == "store":
        out_spec = pl.BlockSpec((1, tile, d_rnn), lambda b, i: (b, t_of(i), 0))
        out_shape = jax.ShapeDtypeStruct((bsz, seq, d_rnn), F32)
    else:
        out_spec = pl.BlockSpec((1, tile, d), lambda b, i: (b, t_of(i), 0))
        out_shape = jax.ShapeDtypeStruct((bsz, seq, d), F32)

    kern = functools.partial(_rglru_kernel, tile=tile, n_tiles=n_tiles, reverse=reverse, mode=mode, halo=halo,
                             gate_starts=gate_starts)
    return pl.pallas_call(
        kern,
        grid=(bsz, n_tiles),
        in_specs=in_specs,
        out_specs=out_spec,
        out_shape=out_shape,
        scratch_shapes=[
            pltpu.VMEM((tile + 2 * RG_HALO, d), F32),
            pltpu.VMEM((tile + 2 * RG_HALO, d_rnn), F32),
            pltpu.VMEM((tile, d_rnn), F32),
            pltpu.VMEM((tile, d_rnn), F32),
            pltpu.VMEM((tile, d_rnn), F32),
            pltpu.VMEM((SUBLANES, d_rnn), F32),
        ],
        compiler_params=pltpu.CompilerParams(
            dimension_semantics=("arbitrary", "arbitrary"), vmem_limit_bytes=VMEM_LIMIT_BYTES),
        name=f"rglru_{mode}_{'bwd' if reverse else 'fwd'}",
    )(*args)


def _ffn_kernel(*refs, tile, n_tiles, width, final):
    refs = list(refs)
    hp_ref, hc_ref, hn_ref, mod_ref, ng_ref, wuv_ref, wug_ref, cwv_ref, cwg_ref, cbv_ref, cbg_ref, wd_ref = refs[:12]
    refs = refs[12:]
    if final:
        nf_ref = refs[0]
        refs = refs[1:]
    out_ref, xm_buf, acc_ref = refs

    ti = pl.program_id(1)
    shift, scale, gate = mod_ref[0, 3:4, :], mod_ref[0, 4:5, :], mod_ref[0, 5:6, :]
    ng = ng_ref[...]
    W = width
    rows = tile + 2 * W

    zero_halo = jnp.zeros((W, xm_buf.shape[1]), F32)
    xm_buf[0:W, :] = jnp.where(ti > 0, _rms_mod(hp_ref[0], ng, scale, shift), zero_halo).astype(BF16)
    xm_buf[W:W + tile, :] = _rms_mod(hc_ref[0], ng, scale, shift).astype(BF16)
    xm_buf[W + tile:, :] = jnp.where(ti < n_tiles - 1, _rms_mod(hn_ref[0], ng, scale, shift), zero_halo).astype(BF16)
    acc_ref[...] = jnp.zeros_like(acc_ref)

    col = lax.broadcasted_iota(jnp.int32, (rows, 1), 0) % W
    has_left = col != 0
    has_right = col != W - 1

    def conv(u, cw_ref, cb_ref, c):
        u_l = jnp.where(has_left, pltpu.roll(u, 1, 0), 0.0)
        u_r = jnp.where(has_right, pltpu.roll(u, rows - 1, 0), 0.0)
        cw = cw_ref[c]
        acc = cb_ref[c]
        for dr in range(3):
            lo = dr * W
            acc = acc + cw[3 * dr:3 * dr + 1, :] * u_l[lo:lo + tile, :]
            acc = acc + cw[3 * dr + 1:3 * dr + 2, :] * u[lo:lo + tile, :]
            acc = acc + cw[3 * dr + 2:3 * dr + 3, :] * u_r[lo:lo + tile, :]
        return acc

    def chunk(c, carry):
        xm = xm_buf[...]
        v = conv(jnp.dot(xm, wuv_ref[c], preferred_element_type=F32), cwv_ref, cbv_ref, c)
        g = conv(jnp.dot(xm, wug_ref[c], preferred_element_type=F32), cwg_ref, cbg_ref, c)
        gated = (g * jax.nn.sigmoid(g) * v).astype(BF16)
        acc_ref[...] += jnp.dot(gated, wd_ref[c], preferred_element_type=F32)
        return carry

    lax.fori_loop(0, wuv_ref.shape[0], chunk, 0)

    h = hc_ref[0] + gate * acc_ref[...]
    if final:
        ms = jnp.mean(h * h, axis=-1, keepdims=True)
        h = h * lax.rsqrt(ms + EPS) * nf_ref[...]
    out_ref[0] = h


def _ffn_call(h, modrows, ng, w_up, conv_w, conv_b, w_down, *, tile, width, norm_final=None):
    bsz, seq, d = h.shape
    d_ff = w_down.shape[0]
    cf = MXU_DIM
    n_chunks = d_ff // cf
    n_tiles = seq // tile
    rpt = tile // width
    n_rows = seq // width

    def chunked_cols(w):
        return w.reshape(w.shape[0], n_chunks, cf).transpose(1, 0, 2)

    cw = conv_w.reshape(9, 2 * d_ff)
    weights = [
        ng,
        chunked_cols(w_up[:, :d_ff]).astype(BF16), chunked_cols(w_up[:, d_ff:]).astype(BF16),
        chunked_cols(cw[:, :d_ff]), chunked_cols(cw[:, d_ff:]),
        chunked_cols(conv_b[None, :d_ff]), chunked_cols(conv_b[None, d_ff:]),
        w_down.reshape(n_chunks, cf, d).astype(BF16),
    ]
    final = norm_final is not None
    if final:
        weights.append(norm_final)

    in_specs = [
        pl.BlockSpec((1, width, d), lambda b, i: (b, jnp.maximum(i * rpt - 1, 0), 0)),
        pl.BlockSpec((1, tile, d), lambda b, i: (b, i, 0)),
        pl.BlockSpec((1, width, d), lambda b, i: (b, jnp.minimum((i + 1) * rpt, n_rows - 1), 0)),
        pl.BlockSpec((1,) + modrows.shape[1:], lambda b, i: (b, 0, 0)),
    ] + [_const_spec(w.shape) for w in weights]

    kern = functools.partial(_ffn_kernel, tile=tile, n_tiles=n_tiles, width=width, final=final)
    return pl.pallas_call(
        kern,
        grid=(bsz, n_tiles),
        in_specs=in_specs,
        out_specs=pl.BlockSpec((1, tile, d), lambda b, i: (b, i, 0)),
        out_shape=jax.ShapeDtypeStruct((bsz, seq, d), F32),
        scratch_shapes=[
            pltpu.VMEM((tile + 2 * width, d), BF16),
            pltpu.VMEM((tile, d), F32),
        ],
        compiler_params=pltpu.CompilerParams(
            dimension_semantics=("arbitrary", "arbitrary"), vmem_limit_bytes=VMEM_LIMIT_BYTES),
        name="conv_ffn_final" if final else "conv_ffn",
    )(h, h, h, modrows, *weights)


def _conformer_kernel(hp_ref, hc_ref, hn_ref, mod_ref, ng_ref, w1a_ref, w1g_ref, b1a_ref, b1g_ref, cw_ref, cb_ref,
                      lng_ref, lnb_ref, w2_ref, b2_ref, out_ref, xm_buf, v_buf, *, tile, n_tiles):
    ti = pl.program_id(1)
    shift, scale, gate = mod_ref[0, 0:1, :], mod_ref[0, 1:2, :], mod_ref[0, 2:3, :]
    ng = ng_ref[...]
    H = CF_HALO

    xm_buf[0:H, :] = _rms_mod(hp_ref[0], ng, scale, shift).astype(BF16)
    xm_buf[H:H + tile, :] = _rms_mod(hc_ref[0], ng, scale, shift).astype(BF16)
    xm_buf[H + tile:, :] = _rms_mod(hn_ref[0], ng, scale, shift).astype(BF16)

    xm = xm_buf[...]
    a = jnp.dot(xm, w1a_ref[...], preferred_element_type=F32) + b1a_ref[...]
    g = jnp.dot(xm, w1g_ref[...], preferred_element_type=F32) + b1g_ref[...]
    v_buf[...] = a * jax.nn.sigmoid(g)

    @pl.when(ti == 0)
    def _():
        v_buf[0:H, :] = jnp.zeros((H, v_buf.shape[1]), F32)

    @pl.when(ti == n_tiles - 1)
    def _():
        v_buf[H + tile:, :] = jnp.zeros((H, v_buf.shape[1]), F32)

    y = cb_ref[...]
    for k in range(cw_ref.shape[0]):
        off = H - CF_CONV_LEFT + k
        y = y + cw_ref[k:k + 1, :] * v_buf[off:off + tile, :]

    mu = jnp.mean(y, axis=-1, keepdims=True)
    yc = y - mu
    var = jnp.mean(yc * yc, axis=-1, keepdims=True)
    yn = yc * lax.rsqrt(var + EPS) * lng_ref[...] + lnb_ref[...]
    s = (yn * jax.nn.sigmoid(yn)).astype(BF16)
    o = jnp.dot(s, w2_ref[...], preferred_element_type=F32) + b2_ref[...]
    out_ref[0] = hc_ref[0] + gate * o


def _conformer_call(h, modrows, ng, w1, b1, cw, cb, lng, lnb, w2, b2, *, tile):
    bsz, seq, d = h.shape
    n_tiles = seq // tile
    hpt = tile // CF_HALO
    n_hblk = seq // CF_HALO
    weights = [ng, w1[:, :d].astype(BF16), w1[:, d:].astype(BF16), b1[None, :d], b1[None, d:], cw, cb[None],
               lng[None], lnb[None], w2.astype(BF16), b2[None]]
    in_specs = [
        pl.BlockSpec((1, CF_HALO, d), lambda b, i: (b, jnp.maximum(i * hpt - 1, 0), 0)),
        pl.BlockSpec((1, tile, d), lambda b, i: (b, i, 0)),
        pl.BlockSpec((1, CF_HALO, d), lambda b, i: (b, jnp.minimum((i + 1) * hpt, n_hblk - 1), 0)),
        pl.BlockSpec((1,) + modrows.shape[1:], lambda b, i: (b, 0, 0)),
    ] + [_const_spec(w.shape) for w in weights]
    kern = functools.partial(_conformer_kernel, tile=tile, n_tiles=n_tiles)
    return pl.pallas_call(
        kern,
        grid=(bsz, n_tiles),
        in_specs=in_specs,
        out_specs=pl.BlockSpec((1, tile, d), lambda b, i: (b, i, 0)),
        out_shape=jax.ShapeDtypeStruct((bsz, seq, d), F32),
        scratch_shapes=[
            pltpu.VMEM((tile + 2 * CF_HALO, d), BF16),
            pltpu.VMEM((tile + 2 * CF_HALO, d), F32),
        ],
        compiler_params=pltpu.CompilerParams(
            dimension_semantics=("arbitrary", "arbitrary"), vmem_limit_bytes=VMEM_LIMIT_BYTES),
        name="conformer_conv",
    )(h, h, h, modrows, *weights)


def _tiles(seq):
    return dict(
        rglru=min(512, seq),
        ffn=min(512, seq),
        conformer=min(512, seq),
    )


def kernel(x, c, ctx, c_ctx, ada_w, ada_b, norm_mix, norm_ffn, rg_w_in, rg_conv_w, rg_conv_b, rg_wa, rg_ba, rg_wx, rg_bx, rg_lam, rg_w_out, cf_w_pw1, cf_b_pw1, cf_conv_w, cf_conv_b, cf_ln_g, cf_ln_b, cf_w_pw2, cf_b_pw2, ffn_w_up, ffn_conv_w, ffn_conv_b, ffn_w_down, norm_final):
    bsz, seq, d = x.shape
    depth = ada_w.shape[0]
    assert depth == 2, "layer 0 = RG-LRU, layer 1 = Conformer conv; the context stream feeds layer 0 only"
    assert seq % GRID_W == 0
    d_rnn = rg_w_in.shape[2] // 2
    n_heads, head_dim = rg_wa.shape[2], rg_wa.shape[3]
    tiles = _tiles(seq)
    ctx_len = ctx.shape[1]

    n_mod_rows = -(-(bsz + 1) // SUBLANES) * SUBLANES
    cc = jnp.concatenate([c, c_ctx[None], jnp.zeros((n_mod_rows - bsz - 1, d), F32)], axis=0)
    mod = _modulation(cc, ada_w, ada_b)

    def modrows(layer, rows):
        m = mod[layer][rows].reshape(-1, 6, d)
        return jnp.concatenate([m, jnp.zeros((m.shape[0], SUBLANES - 6, d), F32)], axis=1)

    lat_rows = jnp.arange(bsz)
    ctx_rows = jnp.full((bsz,), bsz)

    k = 0
    gate_starts = _gate_windows(d_rnn, head_dim)
    ng = norm_mix[0][None]
    winx = rg_w_in[k][:, d_rnn:].astype(BF16)
    wing = rg_w_in[k][:, :d_rnn].astype(BF16)
    cw, cb = rg_conv_w[k], rg_conv_b[k][None]
    lam = rg_lam[k]
    zero_state = jnp.zeros((bsz, 1, d_rnn), F32)
    per_dir = []
    for z in range(2):
        per_dir.append(dict(
            wg=_pack_gate_weights(rg_wa[k, z], rg_wx[k, z], gate_starts),
            ba=rg_ba[k, z].reshape(1, d_rnn), bx=rg_bx[k, z].reshape(1, d_rnn), lam=lam[z][None]))

    def rg(inp, mrows, z, h0, mode, tile, **kw):
        p = per_dir[z]
        return _rglru_call(inp, mrows, ng, winx, cw, cb, p["wg"], p["ba"], p["bx"], p["lam"], h0,
                           tile=tile, reverse=bool(z), mode=mode, gate_starts=gate_starts, **kw)

    ctx_mod = modrows(0, ctx_rows)
    lat_mod0 = modrows(0, lat_rows)
    hf0 = rg(ctx, ctx_mod, 0, zero_state, "state", ctx_len)
    hb0 = rg(ctx, ctx_mod, 1, zero_state, "state", ctx_len)
    hb = rg(x, lat_mod0, 1, hb0, "store", tiles["rglru"])
    h = rg(x, lat_mod0, 0, hf0, "out", tiles["rglru"], hb=hb, wing=wing, wout=rg_w_out[k].astype(BF16))

    h = _ffn_call(h, lat_mod0, norm_ffn[0][None], ffn_w_up[0], ffn_conv_w[0], ffn_conv_b[0], ffn_w_down[0],
                  tile=tiles["ffn"], width=GRID_W)

    lat_mod1 = modrows(1, lat_rows)
    h = _conformer_call(h, lat_mod1, norm_mix[1][None], cf_w_pw1[0], cf_b_pw1[0], cf_conv_w[0], cf_conv_b[0],
                        cf_ln_g[0], cf_ln_b[0], cf_w_pw2[0], cf_b_pw2[0], tile=tiles["conformer"])
    h = _ffn_call(h, lat_mod1, norm_ffn[1][None], ffn_w_up[1], ffn_conv_w[1], ffn_conv_b[1], ffn_w_down[1],
                  tile=tiles["ffn"], width=GRID_W, norm_final=norm_final[None])
    return h
```

```python
import functools

import jax
import jax.numpy as jnp
from jax import lax
from jax.experimental import pallas as pl
from jax.experimental.pallas import tpu as pltpu

F32 = jnp.float32
BF16 = jnp.bfloat16

EPS = 1e-6
RG_C = 8.0
GRID_W = 64
RG_CONV_LEFT = 2
CF_CONV_LEFT = 15

SUBLANES = 8
LANES = 128
MXU_DIM = 256
VMEM_LIMIT_BYTES = 56 * 1024 * 1024

RG_HALO = SUBLANES
CF_HALO = 2 * SUBLANES
GATE_K = 2 * MXU_DIM
CF_ROW_BLOCK = 64
BF16_ROWS = 2 * SUBLANES
FFN_SLABS = 8


def _const_spec(shape):
    nd = len(shape)
    return pl.BlockSpec(shape, lambda *_: (0,) * nd, pipeline_mode=pl.Buffered(1))


def _rms_mod(x, gs, shift):
    ms = jnp.mean(x * x, axis=-1, keepdims=True)
    return x * lax.rsqrt(ms + EPS) * gs + shift


def _mod_kernel(cc_ref, w_ref, b_ref, o_ref):
    s = cc_ref[...]
    s = s * jax.nn.sigmoid(s)
    o_ref[0] = jnp.dot(s, w_ref[0], preferred_element_type=F32) + b_ref[0]


def _modulation(cc, ada_w, ada_b):
    depth, d, n = ada_w.shape
    rows = cc.shape[0]
    nc = n // 6
    return pl.pallas_call(
        _mod_kernel,
        grid=(depth, n // nc),
        in_specs=[
            pl.BlockSpec((rows, d), lambda i, j: (0, 0)),
            pl.BlockSpec((1, d, nc), lambda i, j: (i, 0, j)),
            pl.BlockSpec((1, 1, nc), lambda i, j: (i, 0, j)),
        ],
        out_specs=pl.BlockSpec((1, rows, nc), lambda i, j: (i, 0, j)),
        out_shape=jax.ShapeDtypeStruct((depth, rows, n), F32),
        compiler_params=pltpu.CompilerParams(
            dimension_semantics=("arbitrary", "arbitrary"), vmem_limit_bytes=VMEM_LIMIT_BYTES),
        name="adaln_mod",
    )(cc, ada_w, ada_b.reshape(depth, 1, n))


def _gate_windows(d_rnn, head_dim):
    starts = []
    for j in range(d_rnn // MXU_DIM):
        first_row = (j * MXU_DIM // head_dim) * head_dim
        last_row = ((j * MXU_DIM + MXU_DIM - 1) // head_dim + 1) * head_dim
        k0 = min(first_row // LANES * LANES, d_rnn - GATE_K)
        assert k0 <= first_row and last_row <= k0 + GATE_K
        starts.append(k0)
    return tuple(starts)


def _pack_gate_weights(wa, wx, starts):
    dense_a = jax.scipy.linalg.block_diag(*[wa[h] for h in range(wa.shape[0])])
    dense_x = jax.scipy.linalg.block_diag(*[wx[h] for h in range(wx.shape[0])])
    groups = []
    for j, k0 in enumerate(starts):
        cols = slice(j * MXU_DIM, (j + 1) * MXU_DIM)
        groups.append(jnp.concatenate([dense_a[k0:k0 + GATE_K, cols], dense_x[k0:k0 + GATE_K, cols]], axis=1))
    return jnp.stack(groups).astype(BF16)


def _rglru_kernel(*refs, tile, n_tiles, reverse, mode, halo, gate_starts):
    refs = list(refs)
    if halo:
        xp_ref, xc_ref, xn_ref = refs[:3]
        refs = refs[3:]
    else:
        xc_ref = refs[0]
        refs = refs[1:]
    (mod_ref, ng_ref, winx_ref, cw_ref, cb_ref, wg_ref, ba_ref, bx_ref, lam_ref, h0_ref) = refs[:10]
    refs = refs[10:]
    if mode == "out":
        hb_ref, wing_ref, wout_ref = refs[:3]
        refs = refs[3:]
    out_ref, xm_buf, ux_buf, uh_buf, a_buf, b_buf, carry_ref = refs

    step = pl.program_id(1)
    ti = (n_tiles - 1 - step) if reverse else step
    shift, scale, gate = mod_ref[0, 0:1, :], mod_ref[0, 1:2, :], mod_ref[0, 2:3, :]
    gs = ng_ref[...] * (1.0 + scale)
    H = RG_HALO
    d_rnn = ux_buf.shape[1]

    xm_buf[H:H + tile, :] = _rms_mod(xc_ref[0], gs, shift)
    zero_halo = jnp.zeros((H, xm_buf.shape[1]), F32)
    if halo:
        xm_buf[0:H, :] = jnp.where(ti > 0, _rms_mod(xp_ref[0], gs, shift), zero_halo)
        xm_buf[H + tile:, :] = jnp.where(ti < n_tiles - 1, _rms_mod(xn_ref[0], gs, shift), zero_halo)
    else:
        xm_buf[0:H, :] = zero_halo
        xm_buf[H + tile:, :] = zero_halo

    ux_buf[...] = jnp.dot(xm_buf[...].astype(BF16), winx_ref[...], preferred_element_type=F32)

    uh = cb_ref[...]
    for k in range(cw_ref.shape[0]):
        off = H - RG_CONV_LEFT + k
        uh = uh + cw_ref[k:k + 1, :] * ux_buf[off:off + tile, :]
    uh_buf[...] = uh

    softplus_neg_lam = jax.nn.softplus(-lam_ref[...])
    for j, k0 in enumerate(gate_starts):
        cols = slice(j * MXU_DIM, (j + 1) * MXU_DIM)
        pre = jnp.dot(uh_buf[:, k0:k0 + GATE_K].astype(BF16), wg_ref[j], preferred_element_type=F32)
        r = jax.nn.sigmoid(pre[:, :MXU_DIM] + ba_ref[:, cols])
        ig = jax.nn.sigmoid(pre[:, MXU_DIM:] + bx_ref[:, cols])
        log_a = (-RG_C) * r * softplus_neg_lam[:, cols]
        a = jnp.exp(log_a)
        a_buf[:, cols] = a
        b_buf[:, cols] = jnp.sqrt(jnp.tanh(-log_a) * (1.0 + a * a)) * (ig * uh_buf[:, cols])

    @pl.when(step == 0)
    def _():
        carry_ref[...] = jnp.broadcast_to(h0_ref[0], carry_ref.shape)

    row = lax.broadcasted_iota(jnp.int32, (SUBLANES, d_rnn), 0)
    n_blocks = tile // SUBLANES

    def scan_block(q, carry):
        blk = (n_blocks - 1 - q) if reverse else q
        r0 = pl.multiple_of(blk * SUBLANES, SUBLANES)
        a = a_buf[pl.ds(r0, SUBLANES), :]
        b = b_buf[pl.ds(r0, SUBLANES), :]
        for d in (1, 2, 4):
            if reverse:
                keep = row < SUBLANES - d
                sh = SUBLANES - d
            else:
                keep = row >= d
                sh = d
            a_s = jnp.where(keep, pltpu.roll(a, sh, 0), 1.0)
            b_s = jnp.where(keep, pltpu.roll(b, sh, 0), 0.0)
            b = a * b_s + b
            a = a * a_s
        h = b + a * carry
        b_buf[pl.ds(r0, SUBLANES), :] = h
        edge = h[0:1, :] if reverse else h[SUBLANES - 1:SUBLANES, :]
        return jnp.broadcast_to(edge, (SUBLANES, d_rnn))

    carry = lax.fori_loop(0, n_blocks, scan_block, carry_ref[...])
    carry_ref[...] = carry

    if mode == "state":
        out_ref[0] = carry[0:1, :]
    elif mode == "store":
        out_ref[0] = b_buf[...]
    else:
        ug = jnp.dot(xm_buf[H:H + tile, :].astype(BF16), wing_ref[...], preferred_element_type=F32)
        v = ((b_buf[...] + hb_ref[0]) * jax.nn.gelu(ug)).astype(BF16)
        y = jnp.dot(v, wout_ref[...], preferred_element_type=F32)
        out_ref[0] = xc_ref[0] + gate * y


def _rglru_call(x, modrows, ng, winx, cw, cb, wg, ba, bx, lam, h0, *, tile, reverse, mode, gate_starts,
                hb=None, wing=None, wout=None):
    bsz, seq, d = x.shape
    d_rnn = winx.shape[1]
    n_tiles = seq // tile
    halo = n_tiles > 1
    hpt = tile // RG_HALO
    n_hblk = seq // RG_HALO

    def t_of(i):
        return (n_tiles - 1 - i) if reverse else i

    in_specs, args = [], []
    if halo:
        in_specs.append(pl.BlockSpec((1, RG_HALO, d), lambda b, i: (b, jnp.maximum(t_of(i) * hpt - 1, 0), 0)))
        args.append(x)
    in_specs.append(pl.BlockSpec((1, tile, d), lambda b, i: (b, t_of(i), 0)))
    args.append(x)
    if halo:
        in_specs.append(
            pl.BlockSpec((1, RG_HALO, d), lambda b, i: (b, jnp.minimum((t_of(i) + 1) * hpt, n_hblk - 1), 0)))
        args.append(x)
    in_specs.append(pl.BlockSpec((1,) + modrows.shape[1:], lambda b, i: (b, 0, 0)))
    args.append(modrows)
    for w in (ng, winx, cw, cb, wg, ba, bx, lam):
        in_specs.append(_const_spec(w.shape))
        args.append(w)
    in_specs.append(pl.BlockSpec((1, 1, d_rnn), lambda b, i: (b, 0, 0)))
    args.append(h0)
    if mode == "out":
        in_specs.append(pl.BlockSpec((1, tile, d_rnn), lambda b, i: (b, t_of(i), 0)))
        args.append(hb)
        for w in (wing, wout):
            in_specs.append(_const_spec(w.shape))
            args.append(w)

    if mode == "state":
        out_spec = pl.BlockSpec((1, 1, d_rnn), lambda b, i: (b, 0, 0))
        out_shape = jax.ShapeDtypeStruct((bsz, 1, d_rnn), F32)
    elif mode == "store":
        out_spec = pl.BlockSpec((1, tile, d_rnn), lambda b, i: (b, t_of(i), 0))
        out_shape = jax.ShapeDtypeStruct((bsz, seq, d_rnn), F32)
    else:
        out_spec = pl.BlockSpec((1, tile, d), lambda b, i: (b, t_of(i), 0))
        out_shape = jax.ShapeDtypeStruct((bsz, seq, d), F32)

    kern = functools.partial(_rglru_kernel, tile=tile, n_tiles=n_tiles, reverse=reverse, mode=mode, halo=halo,
                             gate_starts=gate_starts)
    return pl.pallas_call(
        kern,
        grid=(bsz, n_tiles),
        in_specs=in_specs,
        out_specs=out_spec,
        out_shape=out_shape,
        scratch_shapes=[
            pltpu.VMEM((tile + 2 * RG_HALO, d), F32),
            pltpu.VMEM((tile + 2 * RG_HALO, d_rnn), F32),
            pltpu.VMEM((tile, d_rnn), F32),
            pltpu.VMEM((tile, d_rnn), F32),
            pltpu.VMEM((tile, d_rnn), F32),
            pltpu.VMEM((SUBLANES, d_rnn), F32),
        ],
        compiler_params=pltpu.CompilerParams(
            dimension_semantics=("arbitrary", "arbitrary"), vmem_limit_bytes=VMEM_LIMIT_BYTES),
        name=f"rglru_{mode}_{'bwd' if reverse else 'fwd'}",
    )(*args)


def _shift_rows_down(x):
    r = pltpu.roll(x, 1, 0)
    sub = lax.broadcasted_iota(jnp.int32, (SUBLANES, x.shape[1]), 0)
    return jnp.concatenate([jnp.where(sub == 0, 0.0, r[:SUBLANES]), r[SUBLANES:]], axis=0)


def _shift_rows_up(x):
    n = x.shape[0]
    r = pltpu.roll(x, n - 1, 0)
    sub = lax.broadcasted_iota(jnp.int32, (SUBLANES, x.shape[1]), 0)
    return jnp.concatenate([r[:n - SUBLANES], jnp.where(sub == SUBLANES - 1, 0.0, r[n - SUBLANES:])], axis=0)


def _ffn_kernel(*refs, tile, n_tiles, width, final):
    refs = list(refs)
    hp_ref, hc_ref, hn_ref, mod_ref, ng_ref, wuv_ref, wug_ref, cwv_ref, cwg_ref, cbv_ref, cbg_ref, wd_ref = refs[:12]
    refs = refs[12:]
    if final:
        nf_ref = refs[0]
        refs = refs[1:]
    out_ref, xm_buf, u_a, u_b, gated_a, gated_b = refs
    u_slots, gated_slots = (u_a, u_b), (gated_a, gated_b)

    ti = pl.program_id(1)
    shift, scale, gate = mod_ref[0, 3:4, :], mod_ref[0, 4:5, :], mod_ref[0, 5:6, :]
    gs = ng_ref[...] * (1.0 + scale)
    W = width
    n_chunks = wuv_ref.shape[0]
    assert n_chunks >= 3

    zero_halo = jnp.zeros((W, xm_buf.shape[1]), F32)
    xm_buf[0:W, :] = jnp.where(ti > 0, _rms_mod(hp_ref[0], gs, shift), zero_halo).astype(BF16)
    xm_buf[W:W + tile, :] = _rms_mod(hc_ref[0], gs, shift).astype(BF16)
    xm_buf[W + tile:, :] = jnp.where(ti < n_tiles - 1, _rms_mod(hn_ref[0], gs, shift), zero_halo).astype(BF16)

    rows = tile + 2 * W
    up_rows, dn_rows, grid_rows = rows // FFN_SLABS, tile // FFN_SLABS, tile // W // FFN_SLABS
    assert up_rows * FFN_SLABS == rows and up_rows % BF16_ROWS == 0 and grid_rows * FFN_SLABS * W == tile
    cf = wuv_ref.shape[2]

    def up_slab(c, slot, i):
        sl = slice(i * up_rows, (i + 1) * up_rows)
        xm = xm_buf[sl, :]
        u_slots[slot][0, sl, :] = jnp.dot(xm, wuv_ref[c], preferred_element_type=F32)
        u_slots[slot][1, sl, :] = jnp.dot(xm, wug_ref[c], preferred_element_type=F32)

    def conv_rows(u_ref, which, cw, cb, rb, lanes):
        taps = [u_ref[which, (rb + dr) * W:(rb + dr + 1) * W, lanes] for dr in range(3)]

        def column(k):
            return (cw[k:k + 1, lanes] * taps[0] + cw[3 + k:4 + k, lanes] * taps[1]
                    + cw[6 + k:7 + k, lanes] * taps[2])

        return cb[:, lanes] + column(1) + _shift_rows_down(column(0)) + _shift_rows_up(column(2))

    def conv_slab(c, slot, i):
        cwv, cwg, cbv, cbg = cwv_ref[c], cwg_ref[c], cbv_ref[c], cbg_ref[c]
        for rb in range(i * grid_rows, (i + 1) * grid_rows):
            for lb in range(cf // LANES):
                lanes = slice(lb * LANES, (lb + 1) * LANES)
                v = conv_rows(u_slots[slot], 0, cwv, cbv, rb, lanes)
                g = conv_rows(u_slots[slot], 1, cwg, cbg, rb, lanes)
                gated_slots[slot][rb * W:(rb + 1) * W, lanes] = (g * jax.nn.sigmoid(g) * v).astype(BF16)

    def down_slab(c, slot, i, first):
        sl = slice(i * dn_rows, (i + 1) * dn_rows)
        y = jnp.dot(gated_slots[slot][sl, :], wd_ref[c], preferred_element_type=F32)
        if first:
            out_ref[0, sl, :] = y
        else:
            out_ref[0, sl, :] += y

    def stage(c_up=None, c_conv=None, c_down=None, first_down=False):
        for i in range(FFN_SLABS):
            if c_up is not None:
                up_slab(c_up[0], c_up[1], i)
            if c_conv is not None:
                conv_slab(c_conv[0], c_conv[1], i)
            if c_down is not None:
                down_slab(c_down[0], c_down[1], i, first_down)

    stage(c_up=(0, 0))
    stage(c_up=(1, 1), c_conv=(0, 0))
    stage(c_up=(2, 0), c_conv=(1, 1), c_down=(0, 0), first_down=True)

    def steady(c, slot):
        stage(c_up=(c + 1, 1 - slot), c_conv=(c, slot), c_down=(c - 1, 1 - slot))

    n_steady = n_chunks - 3

    def steady_pair(p, carry):
        c = 2 + 2 * p
        steady(c, 0)
        steady(c + 1, 1)
        return carry

    lax.fori_loop(0, n_steady // 2, steady_pair, 0)
    if n_steady % 2:
        steady(n_chunks - 2, (n_chunks - 2) % 2)
    last = n_chunks - 1
    stage(c_conv=(last, last % 2), c_down=(last - 1, (last - 1) % 2))
    stage(c_down=(last, last % 2))

    h = hc_ref[0] + gate * out_ref[0]
    if final:
        ms = jnp.mean(h * h, axis=-1, keepdims=True)
        h = h * lax.rsqrt(ms + EPS) * nf_ref[...]
    out_ref[0] = h


def _ffn_call(h, modrows, ng, w_up, conv_w, conv_b, w_down, *, tile, width, norm_final=None):
    bsz, seq, d = h.shape
    d_ff = w_down.shape[0]
    cf = MXU_DIM
    n_chunks = d_ff // cf
    n_tiles = seq // tile
    rpt = tile // width
    n_rows = seq // width

    def chunked_cols(w):
        return w.reshape(w.shape[0], n_chunks, cf).transpose(1, 0, 2)

    cw = conv_w.reshape(9, 2 * d_ff)
    weights = [
        ng,
        chunked_cols(w_up[:, :d_ff]).astype(BF16), chunked_cols(w_up[:, d_ff:]).astype(BF16),
        chunked_cols(cw[:, :d_ff]), chunked_cols(cw[:, d_ff:]),
        chunked_cols(conv_b[None, :d_ff]), chunked_cols(conv_b[None, d_ff:]),
        w_down.reshape(n_chunks, cf, d).astype(BF16),
    ]
    final = norm_final is not None
    if final:
        weights.append(norm_final)

    in_specs = [
        pl.BlockSpec((1, width, d), lambda b, i: (b, jnp.maximum(i * rpt - 1, 0), 0)),
        pl.BlockSpec((1, tile, d), lambda b, i: (b, i, 0)),
        pl.BlockSpec((1, width, d), lambda b, i: (b, jnp.minimum((i + 1) * rpt, n_rows - 1), 0)),
        pl.BlockSpec((1,) + modrows.shape[1:], lambda b, i: (b, 0, 0)),
    ] + [_const_spec(w.shape) for w in weights]

    rows = tile + 2 * width
    kern = functools.partial(_ffn_kernel, tile=tile, n_tiles=n_tiles, width=width, final=final)
    return pl.pallas_call(
        kern,
        grid=(bsz, n_tiles),
        in_specs=in_specs,
        out_specs=pl.BlockSpec((1, tile, d), lambda b, i: (b, i, 0)),
        out_shape=jax.ShapeDtypeStruct((bsz, seq, d), F32),
        scratch_shapes=[
            pltpu.VMEM((rows, d), BF16),
            pltpu.VMEM((2, rows, cf), F32),
            pltpu.VMEM((2, rows, cf), F32),
            pltpu.VMEM((tile, cf), BF16),
            pltpu.VMEM((tile, cf), BF16),
        ],
        compiler_params=pltpu.CompilerParams(
            dimension_semantics=("arbitrary", "arbitrary"), vmem_limit_bytes=VMEM_LIMIT_BYTES),
        name="conv_ffn_final" if final else "conv_ffn",
    )(h, h, h, modrows, *weights)


def _conformer_kernel(hp_ref, hc_ref, hn_ref, mod_ref, ng_ref, w1a_ref, w1g_ref, b1a_ref, b1g_ref, cw_ref, cb_ref,
                      lng_ref, lnb_ref, w2_ref, b2_ref, out_ref, xm_buf, v_buf, y_buf, s_buf, *, tile, n_tiles):
    ti = pl.program_id(1)
    shift, scale, gate = mod_ref[0, 0:1, :], mod_ref[0, 1:2, :], mod_ref[0, 2:3, :]
    gs = ng_ref[...] * (1.0 + scale)
    H = CF_HALO
    d = v_buf.shape[1]

    xm_buf[0:H, :] = _rms_mod(hp_ref[0], gs, shift).astype(BF16)
    xm_buf[H:H + tile, :] = _rms_mod(hc_ref[0], gs, shift).astype(BF16)
    xm_buf[H + tile:, :] = _rms_mod(hn_ref[0], gs, shift).astype(BF16)

    xm = xm_buf[...]
    a = jnp.dot(xm, w1a_ref[...], preferred_element_type=F32) + b1a_ref[...]
    g = jnp.dot(xm, w1g_ref[...], preferred_element_type=F32) + b1g_ref[...]
    v_buf[...] = a * jax.nn.sigmoid(g)

    @pl.when(ti == 0)
    def _():
        v_buf[0:H, :] = jnp.zeros((H, d), F32)

    @pl.when(ti == n_tiles - 1)
    def _():
        v_buf[H + tile:, :] = jnp.zeros((H, d), F32)

    first = H - CF_CONV_LEFT
    n_taps = cw_ref.shape[0]
    RB = CF_ROW_BLOCK
    last_aligned = (first + n_taps - 1) // SUBLANES * SUBLANES
    assert first >= 0 and last_aligned + SUBLANES <= 2 * H and tile % RB == 0

    def row_block(i, carry):
        r0 = pl.multiple_of(i * RB, RB)
        for lb in range(d // MXU_DIM):
            lanes = slice(lb * MXU_DIM, (lb + 1) * MXU_DIM)
            y = None
            for res in range(SUBLANES):
                z = None
                for k in range(n_taps):
                    off = first + k
                    if off % SUBLANES != res:
                        continue
                    term = cw_ref[k:k + 1, lanes] * v_buf[pl.ds(r0 + off - res, RB + SUBLANES), lanes]
                    z = term if z is None else z + term
                if z is not None:
                    zs = z[res:res + RB, :]
                    y = zs if y is None else y + zs
            y_buf[pl.ds(r0, RB), lanes] = y + cb_ref[:, lanes]

        yb = y_buf[pl.ds(r0, RB), :]
        mu = jnp.mean(yb, axis=-1, keepdims=True)
        yc = yb - mu
        var = jnp.mean(yc * yc, axis=-1, keepdims=True)
        yn = yc * lax.rsqrt(var + EPS) * lng_ref[...] + lnb_ref[...]
        s_buf[pl.ds(r0, RB), :] = (yn * jax.nn.sigmoid(yn)).astype(BF16)
        return carry

    lax.fori_loop(0, tile // RB, row_block, 0)

    o = jnp.dot(s_buf[...], w2_ref[...], preferred_element_type=F32) + b2_ref[...]
    out_ref[0] = hc_ref[0] + gate * o


def _conformer_call(h, modrows, ng, w1, b1, cw, cb, lng, lnb, w2, b2, *, tile):
    bsz, seq, d = h.shape
    n_tiles = seq // tile
    hpt = tile // CF_HALO
    n_hblk = seq // CF_HALO
    weights = [ng, w1[:, :d].astype(BF16), w1[:, d:].astype(BF16), b1[None, :d], b1[None, d:], cw, cb[None],
               lng[None], lnb[None], w2.astype(BF16), b2[None]]
    in_specs = [
        pl.BlockSpec((1, CF_HALO, d), lambda b, i: (b, jnp.maximum(i * hpt - 1, 0), 0)),
        pl.BlockSpec((1, tile, d), lambda b, i: (b, i, 0)),
        pl.BlockSpec((1, CF_HALO, d), lambda b, i: (b, jnp.minimum((i + 1) * hpt, n_hblk - 1), 0)),
        pl.BlockSpec((1,) + modrows.shape[1:], lambda b, i: (b, 0, 0)),
    ] + [_const_spec(w.shape) for w in weights]
    kern = functools.partial(_conformer_kernel, tile=tile, n_tiles=n_tiles)
    return pl.pallas_call(
        kern,
        grid=(bsz, n_tiles),
        in_specs=in_specs,
        out_specs=pl.BlockSpec((1, tile, d), lambda b, i: (b, i, 0)),
        out_shape=jax.ShapeDtypeStruct((bsz, seq, d), F32),
        scratch_shapes=[
            pltpu.VMEM((tile + 2 * CF_HALO, d), BF16),
            pltpu.VMEM((tile + 2 * CF_HALO, d), F32),
            pltpu.VMEM((tile, d), F32),
            pltpu.VMEM((tile, d), BF16),
        ],
        compiler_params=pltpu.CompilerParams(
            dimension_semantics=("arbitrary", "arbitrary"), vmem_limit_bytes=VMEM_LIMIT_BYTES),
        name="conformer_conv",
    )(h, h, h, modrows, *weights)


def _tiles(seq):
    return dict(
        rglru=min(512, seq),
        ffn=min(1024, seq),
        conformer=min(512, seq),
    )


def kernel(x, c, ctx, c_ctx, ada_w, ada_b, norm_mix, norm_ffn, rg_w_in, rg_conv_w, rg_conv_b, rg_wa, rg_ba, rg_wx, rg_bx, rg_lam, rg_w_out, cf_w_pw1, cf_b_pw1, cf_conv_w, cf_conv_b, cf_ln_g, cf_ln_b, cf_w_pw2, cf_b_pw2, ffn_w_up, ffn_conv_w, ffn_conv_b, ffn_w_down, norm_final):
    bsz, seq, d = x.shape
    depth = ada_w.shape[0]
    assert depth == 2, "layer 0 = RG-LRU, layer 1 = Conformer conv; the context stream feeds layer 0 only"
    assert seq % GRID_W == 0
    d_rnn = rg_w_in.shape[2] // 2
    head_dim = rg_wa.shape[3]
    tiles = _tiles(seq)
    ctx_len = ctx.shape[1]

    n_mod_rows = -(-(bsz + 1) // SUBLANES) * SUBLANES
    cc = jnp.concatenate([c, c_ctx[None], jnp.zeros((n_mod_rows - bsz - 1, d), F32)], axis=0)
    mod = _modulation(cc, ada_w, ada_b)

    def modrows(layer, rows):
        m = mod[layer][rows].reshape(-1, 6, d)
        return jnp.concatenate([m, jnp.zeros((m.shape[0], SUBLANES - 6, d), F32)], axis=1)

    lat_rows = jnp.arange(bsz)
    ctx_rows = jnp.full((bsz,), bsz)

    k = 0
    gate_starts = _gate_windows(d_rnn, head_dim)
    ng = norm_mix[0][None]
    winx = rg_w_in[k][:, d_rnn:].astype(BF16)
    wing = rg_w_in[k][:, :d_rnn].astype(BF16)
    cw, cb = rg_conv_w[k], rg_conv_b[k][None]
    lam = rg_lam[k]
    zero_state = jnp.zeros((bsz, 1, d_rnn), F32)
    per_dir = []
    for z in range(2):
        per_dir.append(dict(
            wg=_pack_gate_weights(rg_wa[k, z], rg_wx[k, z], gate_starts),
            ba=rg_ba[k, z].reshape(1, d_rnn), bx=rg_bx[k, z].reshape(1, d_rnn), lam=lam[z][None]))

    def rg(inp, mrows, z, h0, mode, tile, **kw):
        p = per_dir[z]
        return _rglru_call(inp, mrows, ng, winx, cw, cb, p["wg"], p["ba"], p["bx"], p["lam"], h0,
                           tile=tile, reverse=bool(z), mode=mode, gate_starts=gate_starts, **kw)

    ctx_mod = modrows(0, ctx_rows)
    lat_mod0 = modrows(0, lat_rows)
    hf0 = rg(ctx, ctx_mod, 0, zero_state, "state", ctx_len)
    hb0 = rg(ctx, ctx_mod, 1, zero_state, "state", ctx_len)
    hb = rg(x, lat_mod0, 1, hb0, "store", tiles["rglru"])
    h = rg(x, lat_mod0, 0, hf0, "out", tiles["rglru"], hb=hb, wing=wing, wout=rg_w_out[k].astype(BF16))

    h = _ffn_call(h, lat_mod0, norm_ffn[0][None], ffn_w_up[0], ffn_conv_w[0], ffn_conv_b[0], ffn_w_down[0],
                  tile=tiles["ffn"], width=GRID_W)

    lat_mod1 = modrows(1, lat_rows)
    h = _conformer_call(h, lat_mod1, norm_mix[1][None], cf_w_pw1[0], cf_b_pw1[0], cf_conv_w[0], cf_conv_b[0],
                        cf_ln_g[0], cf_ln_b[0], cf_w_pw2[0], cf_b_pw2[0], tile=tiles["conformer"])
    h = _ffn_call(h, lat_mod1, norm_ffn[1][None], ffn_w_up[1], ffn_conv_w[1], ffn_conv_b[1], ffn_w_down[1],
                  tile=tiles["ffn"], width=GRID_W, norm_final=norm_final[None])
    return h
```

```python
import functools

import jax
import jax.numpy as jnp
from jax import lax
from jax.experimental import pallas as pl
from jax.experimental.pallas import tpu as pltpu

F32 = jnp.float32
BF16 = jnp.bfloat16

EPS = 1e-6
RG_C = 8.0
GRID_W = 64
RG_CONV_LEFT = 2
CF_CONV_LEFT = 15

SUBLANES = 8
LANES = 128
MXU_DIM = 256
VMEM_LIMIT_BYTES = 56 * 1024 * 1024

RG_HALO = SUBLANES
CF_HALO = 2 * SUBLANES
GATE_K = 2 * MXU_DIM
CF_ROW_BLOCK = 64
BF16_ROWS = 2 * SUBLANES
FFN_SLAB_GRID_ROWS = 2


def _const_spec(shape):
    nd = len(shape)
    return pl.BlockSpec(shape, lambda *_: (0,) * nd, pipeline_mode=pl.Buffered(1))


def _rms_mod(x, gs, shift):
    ms = jnp.mean(x * x, axis=-1, keepdims=True)
    return x * lax.rsqrt(ms + EPS) * gs + shift


def _mod_kernel(cc_ref, w_ref, b_ref, o_ref):
    s = cc_ref[...]
    s = s * jax.nn.sigmoid(s)
    o_ref[0] = jnp.dot(s, w_ref[0], preferred_element_type=F32) + b_ref[0]


def _modulation(cc, ada_w, ada_b):
    depth, d, n = ada_w.shape
    rows = cc.shape[0]
    nc = n // 6
    return pl.pallas_call(
        _mod_kernel,
        grid=(depth, n // nc),
        in_specs=[
            pl.BlockSpec((rows, d), lambda i, j: (0, 0)),
            pl.BlockSpec((1, d, nc), lambda i, j: (i, 0, j)),
            pl.BlockSpec((1, 1, nc), lambda i, j: (i, 0, j)),
        ],
        out_specs=pl.BlockSpec((1, rows, nc), lambda i, j: (i, 0, j)),
        out_shape=jax.ShapeDtypeStruct((depth, rows, n), F32),
        compiler_params=pltpu.CompilerParams(
            dimension_semantics=("arbitrary", "arbitrary"), vmem_limit_bytes=VMEM_LIMIT_BYTES),
        name="adaln_mod",
    )(cc, ada_w, ada_b.reshape(depth, 1, n))


def _gate_windows(d_rnn, head_dim):
    starts = []
    for j in range(d_rnn // MXU_DIM):
        first_row = (j * MXU_DIM // head_dim) * head_dim
        last_row = ((j * MXU_DIM + MXU_DIM - 1) // head_dim + 1) * head_dim
        k0 = min(first_row // LANES * LANES, d_rnn - GATE_K)
        assert k0 <= first_row and last_row <= k0 + GATE_K
        starts.append(k0)
    return tuple(starts)


def _pack_gate_weights(wa, wx, starts):
    dense_a = jax.scipy.linalg.block_diag(*[wa[h] for h in range(wa.shape[0])])
    dense_x = jax.scipy.linalg.block_diag(*[wx[h] for h in range(wx.shape[0])])
    groups = []
    for j, k0 in enumerate(starts):
        cols = slice(j * MXU_DIM, (j + 1) * MXU_DIM)
        groups.append(jnp.concatenate([dense_a[k0:k0 + GATE_K, cols], dense_x[k0:k0 + GATE_K, cols]], axis=1))
    return jnp.stack(groups).astype(BF16)


def _rglru_kernel(*refs, tile, n_tiles, reverse, mode, halo, gate_starts):
    refs = list(refs)
    if halo:
        xp_ref, xc_ref, xn_ref = refs[:3]
        refs = refs[3:]
    else:
        xc_ref = refs[0]
        refs = refs[1:]
    (mod_ref, ng_ref, winx_ref, cw_ref, cb_ref, wg_ref, ba_ref, bx_ref, lam_ref, h0_ref) = refs[:10]
    refs = refs[10:]
    if mode == "out":
        hb_ref, wing_ref, wout_ref = refs[:3]
        refs = refs[3:]
    out_ref, xm_buf, ux_buf, uh_buf, a_buf, b_buf, carry_ref = refs

    step = pl.program_id(1)
    ti = (n_tiles - 1 - step) if reverse else step
    shift, scale, gate = mod_ref[0, 0:1, :], mod_ref[0, 1:2, :], mod_ref[0, 2:3, :]
    gs = ng_ref[...] * (1.0 + scale)
    H = RG_HALO
    d_rnn = ux_buf.shape[1]

    xm_buf[H:H + tile, :] = _rms_mod(xc_ref[0], gs, shift)
    zero_halo = jnp.zeros((H, xm_buf.shape[1]), F32)
    if halo:
        xm_buf[0:H, :] = jnp.where(ti > 0, _rms_mod(xp_ref[0], gs, shift), zero_halo)
        xm_buf[H + tile:, :] = jnp.where(ti < n_tiles - 1, _rms_mod(xn_ref[0], gs, shift), zero_halo)
    else:
        xm_buf[0:H, :] = zero_halo
        xm_buf[H + tile:, :] = zero_halo

    ux_buf[...] = jnp.dot(xm_buf[...].astype(BF16), winx_ref[...], preferred_element_type=F32)

    uh = cb_ref[...]
    for k in range(cw_ref.shape[0]):
        off = H - RG_CONV_LEFT + k
        uh = uh + cw_ref[k:k + 1, :] * ux_buf[off:off + tile, :]
    uh_buf[...] = uh

    softplus_neg_lam = jax.nn.softplus(-lam_ref[...])
    for j, k0 in enumerate(gate_starts):
        cols = slice(j * MXU_DIM, (j + 1) * MXU_DIM)
        pre = jnp.dot(uh_buf[:, k0:k0 + GATE_K].astype(BF16), wg_ref[j], preferred_element_type=F32)
        r = jax.nn.sigmoid(pre[:, :MXU_DIM] + ba_ref[:, cols])
        ig = jax.nn.sigmoid(pre[:, MXU_DIM:] + bx_ref[:, cols])
        log_a = (-RG_C) * r * softplus_neg_lam[:, cols]
        a = jnp.exp(log_a)
        a_buf[:, cols] = a
        b_buf[:, cols] = jnp.sqrt(jnp.tanh(-log_a) * (1.0 + a * a)) * (ig * uh_buf[:, cols])

    @pl.when(step == 0)
    def _():
        carry_ref[...] = jnp.broadcast_to(h0_ref[0], carry_ref.shape)

    row = lax.broadcasted_iota(jnp.int32, (SUBLANES, d_rnn), 0)
    n_blocks = tile // SUBLANES

    def scan_block(q, carry):
        blk = (n_blocks - 1 - q) if reverse else q
        r0 = pl.multiple_of(blk * SUBLANES, SUBLANES)
        a = a_buf[pl.ds(r0, SUBLANES), :]
        b = b_buf[pl.ds(r0, SUBLANES), :]
        for d in (1, 2, 4):
            if reverse:
                keep = row < SUBLANES - d
                sh = SUBLANES - d
            else:
                keep = row >= d
                sh = d
            a_s = jnp.where(keep, pltpu.roll(a, sh, 0), 1.0)
            b_s = jnp.where(keep, pltpu.roll(b, sh, 0), 0.0)
            b = a * b_s + b
            a = a * a_s
        h = b + a * carry
        b_buf[pl.ds(r0, SUBLANES), :] = h
        edge = h[0:1, :] if reverse else h[SUBLANES - 1:SUBLANES, :]
        return jnp.broadcast_to(edge, (SUBLANES, d_rnn))

    carry = lax.fori_loop(0, n_blocks, scan_block, carry_ref[...])
    carry_ref[...] = carry

    if mode == "state":
        out_ref[0] = carry[0:1, :]
    elif mode == "store":
        out_ref[0] = b_buf[...]
    else:
        ug = jnp.dot(xm_buf[H:H + tile, :].astype(BF16), wing_ref[...], preferred_element_type=F32)
        v = ((b_buf[...] + hb_ref[0]) * jax.nn.gelu(ug)).astype(BF16)
        y = jnp.dot(v, wout_ref[...], preferred_element_type=F32)
        out_ref[0] = xc_ref[0] + gate * y


def _rglru_call(x, modrows, ng, winx, cw, cb, wg, ba, bx, lam, h0, *, tile, reverse, mode, gate_starts,
                hb=None, wing=None, wout=None):
    bsz, seq, d = x.shape
    d_rnn = winx.shape[1]
    n_tiles = seq // tile
    halo = n_tiles > 1
    hpt = tile // RG_HALO
    n_hblk = seq // RG_HALO

    def t_of(i):
        return (n_tiles - 1 - i) if reverse else i

    in_specs, args = [], []
    if halo:
        in_specs.append(pl.BlockSpec((1, RG_HALO, d), lambda b, i: (b, jnp.maximum(t_of(i) * hpt - 1, 0), 0)))
        args.append(x)
    in_specs.append(pl.BlockSpec((1, tile, d), lambda b, i: (b, t_of(i), 0)))
    args.append(x)
    if halo:
        in_specs.append(
            pl.BlockSpec((1, RG_HALO, d), lambda b, i: (b, jnp.minimum((t_of(i) + 1) * hpt, n_hblk - 1), 0)))
        args.append(x)
    in_specs.append(pl.BlockSpec((1,) + modrows.shape[1:], lambda b, i: (b, 0, 0)))
    args.append(modrows)
    for w in (ng, winx, cw, cb, wg, ba, bx, lam):
        in_specs.append(_const_spec(w.shape))
        args.append(w)
    in_specs.append(pl.BlockSpec((1, 1, d_rnn), lambda b, i: (b, 0, 0)))
    args.append(h0)
    if mode == "out":
        in_specs.append(pl.BlockSpec((1, tile, d_rnn), lambda b, i: (b, t_of(i), 0)))
        args.append(hb)
        for w in (wing, wout):
            in_specs.append(_const_spec(w.shape))
            args.append(w)

    if mode == "state":
        out_spec = pl.BlockSpec((1, 1, d_rnn), lambda b, i: (b, 0, 0))
        out_shape = jax.ShapeDtypeStruct((bsz, 1, d_rnn), F32)
    elif mode == "store":
        out_spec = pl.BlockSpec((1, tile, d_rnn), lambda b, i: (b, t_of(i), 0))
        out_shape = jax.ShapeDtypeStruct((bsz, seq, d_rnn), F32)
    else:
        out_spec = pl.BlockSpec((1, tile, d), lambda b, i: (b, t_of(i), 0))
        out_shape = jax.ShapeDtypeStruct((bsz, seq, d), F32)

    kern = functools.partial(_rglru_kernel, tile=tile, n_tiles=n_tiles, reverse=reverse, mode=mode, halo=halo,
                             gate_starts=gate_starts)
    return pl.pallas_call(
        kern,
        grid=(bsz, n_tiles),
        in_specs=in_specs,
        out_specs=out_spec,
        out_shape=out_shape,
        scratch_shapes=[
            pltpu.VMEM((tile + 2 * RG_HALO, d), F32),
            pltpu.VMEM((tile + 2 * RG_HALO, d_rnn), F32),
            pltpu.VMEM((tile, d_rnn), F32),
            pltpu.VMEM((tile, d_rnn), F32),
            pltpu.VMEM((tile, d_rnn), F32),
            pltpu.VMEM((SUBLANES, d_rnn), F32),
        ],
        compiler_params=pltpu.CompilerParams(
            dimension_semantics=("arbitrary", "arbitrary"), vmem_limit_bytes=VMEM_LIMIT_BYTES),
        name=f"rglru_{mode}_{'bwd' if reverse else 'fwd'}",
    )(*args)


def _shift_rows_down(x):
    r = pltpu.roll(x, 1, 0)
    sub = lax.broadcasted_iota(jnp.int32, (SUBLANES, x.shape[1]), 0)
    return jnp.concatenate([jnp.where(sub == 0, 0.0, r[:SUBLANES]), r[SUBLANES:]], axis=0)


def _shift_rows_up(x):
    n = x.shape[0]
    r = pltpu.roll(x, n - 1, 0)
    sub = lax.broadcasted_iota(jnp.int32, (SUBLANES, x.shape[1]), 0)
    return jnp.concatenate([r[:n - SUBLANES], jnp.where(sub == SUBLANES - 1, 0.0, r[n - SUBLANES:])], axis=0)


def _ffn_kernel(*refs, tile, n_tiles, width, final):
    refs = list(refs)
    hp_ref, hc_ref, hn_ref, mod_ref, ng_ref, wuv_ref, wug_ref, cwv_ref, cwg_ref, cbv_ref, cbg_ref, wd_ref = refs[:12]
    refs = refs[12:]
    if final:
        nf_ref = refs[0]
        refs = refs[1:]
    out_ref, xm_buf, u_a, u_b, gated_a, gated_b = refs
    u_slots, gated_slots = (u_a, u_b), (gated_a, gated_b)

    ti = pl.program_id(1)
    shift, scale, gate = mod_ref[0, 3:4, :], mod_ref[0, 4:5, :], mod_ref[0, 5:6, :]
    gs = ng_ref[...] * (1.0 + scale)
    W = width
    n_chunks, _, cf = wuv_ref.shape
    n_grid_rows = tile // W
    slab = FFN_SLAB_GRID_ROWS * W
    assert n_chunks >= 3 and tile % slab == 0 and 2 * W == slab
    top0, bot0 = tile, tile + W

    zero_halo = jnp.zeros((W, xm_buf.shape[1]), F32)
    xm_buf[0:tile, :] = _rms_mod(hc_ref[0], gs, shift).astype(BF16)
    xm_buf[top0:top0 + W, :] = jnp.where(ti > 0, _rms_mod(hp_ref[0], gs, shift), zero_halo).astype(BF16)
    xm_buf[bot0:bot0 + W, :] = jnp.where(ti < n_tiles - 1, _rms_mod(hn_ref[0], gs, shift), zero_halo).astype(BF16)

    def grid_row_start(r):
        return top0 if r < 0 else (bot0 if r >= n_grid_rows else r * W)

    def up_slab(c, slot, j):
        xm = xm_buf[j * slab:(j + 1) * slab, :]
        for which, w_ref in ((0, wuv_ref), (1, wug_ref)):
            y = jnp.dot(xm, w_ref[c], preferred_element_type=F32)
            for k in range(slab // W):
                blk = y[k * W:(k + 1) * W, :]
                r0 = j * slab + k * W
                u_slots[slot][which, 0, r0:r0 + W, :] = _shift_rows_down(blk).astype(BF16)
                u_slots[slot][which, 1, r0:r0 + W, :] = blk.astype(BF16)
                u_slots[slot][which, 2, r0:r0 + W, :] = _shift_rows_up(blk).astype(BF16)

    def conv_rows(u_ref, which, cw, cb, r):
        acc = cb
        for dr in range(3):
            r0 = grid_row_start(r + dr - 1)
            for dc in range(3):
                acc = acc + cw[3 * dr + dc:3 * dr + dc + 1, :] * u_ref[which, dc, r0:r0 + W, :]
        return acc

    def conv_slab(c, slot, i):
        cwv, cwg = cwv_ref[c].astype(BF16), cwg_ref[c].astype(BF16)
        cbv, cbg = cbv_ref[c].astype(BF16), cbg_ref[c].astype(BF16)
        for r in range(i * FFN_SLAB_GRID_ROWS, (i + 1) * FFN_SLAB_GRID_ROWS):
            v = conv_rows(u_slots[slot], 0, cwv, cbv, r)
            g = conv_rows(u_slots[slot], 1, cwg, cbg, r)
            gated_slots[slot][r * W:(r + 1) * W, :] = g * jax.nn.sigmoid(g) * v

    def down_slab(c, slot, i, first):
        sl = slice(i * slab, (i + 1) * slab)
        y = jnp.dot(gated_slots[slot][sl, :], wd_ref[c], preferred_element_type=F32)
        if first:
            out_ref[0, sl, :] = y
        else:
            out_ref[0, sl, :] += y

    n_conv_slabs = tile // slab
    n_up_slabs = n_conv_slabs + 1

    def stage(c_up=None, c_conv=None, c_down=None, first_down=False):
        for i in range(n_up_slabs):
            if c_down is not None and i < n_conv_slabs:
                down_slab(c_down[0], c_down[1], i, first_down)
            if c_up is not None:
                up_slab(c_up[0], c_up[1], i)
            if c_conv is not None and i < n_conv_slabs:
                conv_slab(c_conv[0], c_conv[1], i)

    stage(c_up=(0, 0))
    stage(c_up=(1, 1), c_conv=(0, 0))
    stage(c_up=(2, 0), c_conv=(1, 1), c_down=(0, 0), first_down=True)

    def steady(c, slot):
        stage(c_up=(c + 1, 1 - slot), c_conv=(c, slot), c_down=(c - 1, 1 - slot))

    n_steady = n_chunks - 3

    def steady_pair(p, carry):
        c = 2 + 2 * p
        steady(c, 0)
        steady(c + 1, 1)
        return carry

    lax.fori_loop(0, n_steady // 2, steady_pair, 0)
    if n_steady % 2:
        steady(n_chunks - 2, (n_chunks - 2) % 2)
    last = n_chunks - 1
    stage(c_conv=(last, last % 2), c_down=(last - 1, (last - 1) % 2))
    stage(c_down=(last, last % 2))

    h = hc_ref[0] + gate * out_ref[0]
    if final:
        ms = jnp.mean(h * h, axis=-1, keepdims=True)
        h = h * lax.rsqrt(ms + EPS) * nf_ref[...]
    out_ref[0] = h


def _ffn_call(h, modrows, ng, w_up, conv_w, conv_b, w_down, *, tile, width, norm_final=None):
    bsz, seq, d = h.shape
    d_ff = w_down.shape[0]
    cf = MXU_DIM
    n_chunks = d_ff // cf
    n_tiles = seq // tile
    rpt = tile // width
    n_rows = seq // width

    def chunked_cols(w):
        return w.reshape(w.shape[0], n_chunks, cf).transpose(1, 0, 2)

    cw = conv_w.reshape(9, 2 * d_ff)
    weights = [
        ng,
        chunked_cols(w_up[:, :d_ff]).astype(BF16), chunked_cols(w_up[:, d_ff:]).astype(BF16),
        chunked_cols(cw[:, :d_ff]), chunked_cols(cw[:, d_ff:]),
        chunked_cols(conv_b[None, :d_ff]), chunked_cols(conv_b[None, d_ff:]),
        w_down.reshape(n_chunks, cf, d).astype(BF16),
    ]
    final = norm_final is not None
    if final:
        weights.append(norm_final)

    in_specs = [
        pl.BlockSpec((1, width, d), lambda b, i: (b, jnp.maximum(i * rpt - 1, 0), 0)),
        pl.BlockSpec((1, tile, d), lambda b, i: (b, i, 0)),
        pl.BlockSpec((1, width, d), lambda b, i: (b, jnp.minimum((i + 1) * rpt, n_rows - 1), 0)),
        pl.BlockSpec((1,) + modrows.shape[1:], lambda b, i: (b, 0, 0)),
    ] + [_const_spec(w.shape) for w in weights]

    rows = tile + 2 * width
    kern = functools.partial(_ffn_kernel, tile=tile, n_tiles=n_tiles, width=width, final=final)
    return pl.pallas_call(
        kern,
        grid=(bsz, n_tiles),
        in_specs=in_specs,
        out_specs=pl.BlockSpec((1, tile, d), lambda b, i: (b, i, 0)),
        out_shape=jax.ShapeDtypeStruct((bsz, seq, d), F32),
        scratch_shapes=[
            pltpu.VMEM((rows, d), BF16),
            pltpu.VMEM((2, 3, rows, cf), BF16),
            pltpu.VMEM((2, 3, rows, cf), BF16),
            pltpu.VMEM((tile, cf), BF16),
            pltpu.VMEM((tile, cf), BF16),
        ],
        compiler_params=pltpu.CompilerParams(
            dimension_semantics=("arbitrary", "arbitrary"), vmem_limit_bytes=VMEM_LIMIT_BYTES),
        name="conv_ffn_final" if final else "conv_ffn",
    )(h, h, h, modrows, *weights)


def _conformer_kernel(hp_ref, hc_ref, hn_ref, mod_ref, ng_ref, w1a_ref, w1g_ref, b1a_ref, b1g_ref, cw_ref, cb_ref,
                      lng_ref, lnb_ref, w2_ref, b2_ref, out_ref, xm_buf, v_buf, y_buf, s_buf, *, tile, n_tiles):
    ti = pl.program_id(1)
    shift, scale, gate = mod_ref[0, 0:1, :], mod_ref[0, 1:2, :], mod_ref[0, 2:3, :]
    gs = ng_ref[...] * (1.0 + scale)
    H = CF_HALO
    d = v_buf.shape[1]

    xm_buf[0:H, :] = _rms_mod(hp_ref[0], gs, shift).astype(BF16)
    xm_buf[H:H + tile, :] = _rms_mod(hc_ref[0], gs, shift).astype(BF16)
    xm_buf[H + tile:, :] = _rms_mod(hn_ref[0], gs, shift).astype(BF16)

    xm = xm_buf[...]
    a = jnp.dot(xm, w1a_ref[...], preferred_element_type=F32) + b1a_ref[...]
    g = jnp.dot(xm, w1g_ref[...], preferred_element_type=F32) + b1g_ref[...]
    v_buf[...] = a * jax.nn.sigmoid(g)

    @pl.when(ti == 0)
    def _():
        v_buf[0:H, :] = jnp.zeros((H, d), F32)

    @pl.when(ti == n_tiles - 1)
    def _():
        v_buf[H + tile:, :] = jnp.zeros((H, d), F32)

    first = H - CF_CONV_LEFT
    n_taps = cw_ref.shape[0]
    RB = CF_ROW_BLOCK
    last_aligned = (first + n_taps - 1) // SUBLANES * SUBLANES
    assert first >= 0 and last_aligned + SUBLANES <= 2 * H and tile % RB == 0

    def row_block(i, carry):
        r0 = pl.multiple_of(i * RB, RB)
        for lb in range(d // MXU_DIM):
            lanes = slice(lb * MXU_DIM, (lb + 1) * MXU_DIM)
            y = None
            for res in range(SUBLANES):
                z = None
                for k in range(n_taps):
                    off = first + k
                    if off % SUBLANES != res:
                        continue
                    term = cw_ref[k:k + 1, lanes] * v_buf[pl.ds(r0 + off - res, RB + SUBLANES), lanes]
                    z = term if z is None else z + term
                if z is not None:
                    zs = z[res:res + RB, :]
                    y = zs if y is None else y + zs
            y_buf[pl.ds(r0, RB), lanes] = y + cb_ref[:, lanes]

        yb = y_buf[pl.ds(r0, RB), :]
        mu = jnp.mean(yb, axis=-1, keepdims=True)
        yc = yb - mu
        var = jnp.mean(yc * yc, axis=-1, keepdims=True)
        yn = yc * lax.rsqrt(var + EPS) * lng_ref[...] + lnb_ref[...]
        s_buf[pl.ds(r0, RB), :] = (yn * jax.nn.sigmoid(yn)).astype(BF16)
        return carry

    lax.fori_loop(0, tile // RB, row_block, 0)

    o = jnp.dot(s_buf[...], w2_ref[...], preferred_element_type=F32) + b2_ref[...]
    out_ref[0] = hc_ref[0] + gate * o


def _conformer_call(h, modrows, ng, w1, b1, cw, cb, lng, lnb, w2, b2, *, tile):
    bsz, seq, d = h.shape
    n_tiles = seq // tile
    hpt = tile // CF_HALO
    n_hblk = seq // CF_HALO
    weights = [ng, w1[:, :d].astype(BF16), w1[:, d:].astype(BF16), b1[None, :d], b1[None, d:], cw, cb[None],
               lng[None], lnb[None], w2.astype(BF16), b2[None]]
    in_specs = [
        pl.BlockSpec((1, CF_HALO, d), lambda b, i: (b, jnp.maximum(i * hpt - 1, 0), 0)),
        pl.BlockSpec((1, tile, d), lambda b, i: (b, i, 0)),
        pl.BlockSpec((1, CF_HALO, d), lambda b, i: (b, jnp.minimum((i + 1) * hpt, n_hblk - 1), 0)),
        pl.BlockSpec((1,) + modrows.shape[1:], lambda b, i: (b, 0, 0)),
    ] + [_const_spec(w.shape) for w in weights]
    kern = functools.partial(_conformer_kernel, tile=tile, n_tiles=n_tiles)
    return pl.pallas_call(
        kern,
        grid=(bsz, n_tiles),
        in_specs=in_specs,
        out_specs=pl.BlockSpec((1, tile, d), lambda b, i: (b, i, 0)),
        out_shape=jax.ShapeDtypeStruct((bsz, seq, d), F32),
        scratch_shapes=[
            pltpu.VMEM((tile + 2 * CF_HALO, d), BF16),
            pltpu.VMEM((tile + 2 * CF_HALO, d), F32),
            pltpu.VMEM((tile, d), F32),
            pltpu.VMEM((tile, d), BF16),
        ],
        compiler_params=pltpu.CompilerParams(
            dimension_semantics=("arbitrary", "arbitrary"), vmem_limit_bytes=VMEM_LIMIT_BYTES),
        name="conformer_conv",
    )(h, h, h, modrows, *weights)


def _tiles(seq):
    return dict(
        rglru=min(512, seq),
        ffn=min(1024, seq),
        conformer=min(512, seq),
    )


def kernel(x, c, ctx, c_ctx, ada_w, ada_b, norm_mix, norm_ffn, rg_w_in, rg_conv_w, rg_conv_b, rg_wa, rg_ba, rg_wx, rg_bx, rg_lam, rg_w_out, cf_w_pw1, cf_b_pw1, cf_conv_w, cf_conv_b, cf_ln_g, cf_ln_b, cf_w_pw2, cf_b_pw2, ffn_w_up, ffn_conv_w, ffn_conv_b, ffn_w_down, norm_final):
    bsz, seq, d = x.shape
    depth = ada_w.shape[0]
    assert depth == 2, "layer 0 = RG-LRU, layer 1 = Conformer conv; the context stream feeds layer 0 only"
    assert seq % GRID_W == 0
    d_rnn = rg_w_in.shape[2] // 2
    head_dim = rg_wa.shape[3]
    tiles = _tiles(seq)
    ctx_len = ctx.shape[1]

    n_mod_rows = -(-(bsz + 1) // SUBLANES) * SUBLANES
    cc = jnp.concatenate([c, c_ctx[None], jnp.zeros((n_mod_rows - bsz - 1, d), F32)], axis=0)
    mod = _modulation(cc, ada_w, ada_b)

    def modrows(layer, rows):
        m = mod[layer][rows].reshape(-1, 6, d)
        return jnp.concatenate([m, jnp.zeros((m.shape[0], SUBLANES - 6, d), F32)], axis=1)

    lat_rows = jnp.arange(bsz)
    ctx_rows = jnp.full((bsz,), bsz)

    k = 0
    gate_starts = _gate_windows(d_rnn, head_dim)
    ng = norm_mix[0][None]
    winx = rg_w_in[k][:, d_rnn:].astype(BF16)
    wing = rg_w_in[k][:, :d_rnn].astype(BF16)
    cw, cb = rg_conv_w[k], rg_conv_b[k][None]
    lam = rg_lam[k]
    zero_state = jnp.zeros((bsz, 1, d_rnn), F32)
    per_dir = []
    for z in range(2):
        per_dir.append(dict(
            wg=_pack_gate_weights(rg_wa[k, z], rg_wx[k, z], gate_starts),
            ba=rg_ba[k, z].reshape(1, d_rnn), bx=rg_bx[k, z].reshape(1, d_rnn), lam=lam[z][None]))

    def rg(inp, mrows, z, h0, mode, tile, **kw):
        p = per_dir[z]
        return _rglru_call(inp, mrows, ng, winx, cw, cb, p["wg"], p["ba"], p["bx"], p["lam"], h0,
                           tile=tile, reverse=bool(z), mode=mode, gate_starts=gate_starts, **kw)

    ctx_mod = modrows(0, ctx_rows)
    lat_mod0 = modrows(0, lat_rows)
    hf0 = rg(ctx, ctx_mod, 0, zero_state, "state", ctx_len)
    hb0 = rg(ctx, ctx_mod, 1, zero_state, "state", ctx_len)
    hb = rg(x, lat_mod0, 1, hb0, "store", tiles["rglru"])
    h = rg(x, lat_mod0, 0, hf0, "out", tiles["rglru"], hb=hb, wing=wing, wout=rg_w_out[k].astype(BF16))

    h = _ffn_call(h, lat_mod0, norm_ffn[0][None], ffn_w_up[0], ffn_conv_w[0], ffn_conv_b[0], ffn_w_down[0],
                  tile=tiles["ffn"], width=GRID_W)

    lat_mod1 = modrows(1, lat_rows)
    h = _conformer_call(h, lat_mod1, norm_mix[1][None], cf_w_pw1[0], cf_b_pw1[0], cf_conv_w[0], cf_conv_b[0],
                        cf_ln_g[0], cf_ln_b[0], cf_w_pw2[0], cf_b_pw2[0], tile=tiles["conformer"])
    h = _ffn_call(h, lat_mod1, norm_ffn[1][None], ffn_w_up[1], ffn_conv_w[1], ffn_conv_b[1], ffn_w_down[1],
                  tile=tiles["ffn"], width=GRID_W, norm_final=norm_final[None])
    return h
```

```python
import functools

import jax
import jax.numpy as jnp
from jax import lax
from jax.experimental import pallas as pl
from jax.experimental.pallas import tpu as pltpu

F32 = jnp.float32
BF16 = jnp.bfloat16

EPS = 1e-6
RG_C = 8.0
GRID_W = 64
RG_CONV_LEFT = 2
CF_CONV_LEFT = 15

SUBLANES = 8
LANES = 128
MXU_DIM = 256
VMEM_LIMIT_BYTES = 56 * 1024 * 1024

RG_HALO = SUBLANES
CF_HALO = 2 * SUBLANES
GATE_K = 2 * MXU_DIM
CF_ROW_BLOCK = 64
BF16_ROWS = 2 * SUBLANES
FFN_SLAB_GRID_ROWS = 4


def _const_spec(shape):
    nd = len(shape)
    return pl.BlockSpec(shape, lambda *_: (0,) * nd, pipeline_mode=pl.Buffered(1))


def _rms_mod(x, gs, shift):
    ms = jnp.mean(x * x, axis=-1, keepdims=True)
    return x * lax.rsqrt(ms + EPS) * gs + shift


def _mod_kernel(cc_ref, w_ref, b_ref, o_ref):
    s = cc_ref[...]
    s = s * jax.nn.sigmoid(s)
    o_ref[0] = jnp.dot(s, w_ref[0], preferred_element_type=F32) + b_ref[0]


def _modulation(cc, ada_w, ada_b):
    depth, d, n = ada_w.shape
    rows = cc.shape[0]
    nc = n // 6
    return pl.pallas_call(
        _mod_kernel,
        grid=(depth, n // nc),
        in_specs=[
            pl.BlockSpec((rows, d), lambda i, j: (0, 0)),
            pl.BlockSpec((1, d, nc), lambda i, j: (i, 0, j)),
            pl.BlockSpec((1, 1, nc), lambda i, j: (i, 0, j)),
        ],
        out_specs=pl.BlockSpec((1, rows, nc), lambda i, j: (i, 0, j)),
        out_shape=jax.ShapeDtypeStruct((depth, rows, n), F32),
        compiler_params=pltpu.CompilerParams(
            dimension_semantics=("arbitrary", "arbitrary"), vmem_limit_bytes=VMEM_LIMIT_BYTES),
        name="adaln_mod",
    )(cc, ada_w, ada_b.reshape(depth, 1, n))


def _gate_windows(d_rnn, head_dim):
    starts = []
    for j in range(d_rnn // MXU_DIM):
        first_row = (j * MXU_DIM // head_dim) * head_dim
        last_row = ((j * MXU_DIM + MXU_DIM - 1) // head_dim + 1) * head_dim
        k0 = min(first_row // LANES * LANES, d_rnn - GATE_K)
        assert k0 <= first_row and last_row <= k0 + GATE_K
        starts.append(k0)
    return tuple(starts)


def _pack_gate_weights(wa, wx, starts):
    dense_a = jax.scipy.linalg.block_diag(*[wa[h] for h in range(wa.shape[0])])
    dense_x = jax.scipy.linalg.block_diag(*[wx[h] for h in range(wx.shape[0])])
    groups = []
    for j, k0 in enumerate(starts):
        cols = slice(j * MXU_DIM, (j + 1) * MXU_DIM)
        groups.append(jnp.concatenate([dense_a[k0:k0 + GATE_K, cols], dense_x[k0:k0 + GATE_K, cols]], axis=1))
    return jnp.stack(groups).astype(BF16)


def _rglru_kernel(*refs, tile, n_tiles, reverse, mode, halo, gate_starts):
    refs = list(refs)
    if halo:
        xp_ref, xc_ref, xn_ref = refs[:3]
        refs = refs[3:]
    else:
        xc_ref = refs[0]
        refs = refs[1:]
    (mod_ref, ng_ref, winx_ref, cw_ref, cb_ref, wg_ref, ba_ref, bx_ref, lam_ref, h0_ref) = refs[:10]
    refs = refs[10:]
    if mode == "out":
        hb_ref, wing_ref, wout_ref = refs[:3]
        refs = refs[3:]
    out_ref, xm_buf, ux_buf, uh_buf, a_buf, b_buf, carry_ref = refs

    step = pl.program_id(1)
    ti = (n_tiles - 1 - step) if reverse else step
    shift, scale, gate = mod_ref[0, 0:1, :], mod_ref[0, 1:2, :], mod_ref[0, 2:3, :]
    gs = ng_ref[...] * (1.0 + scale)
    H = RG_HALO
    d_rnn = ux_buf.shape[1]

    xm_buf[H:H + tile, :] = _rms_mod(xc_ref[0], gs, shift)
    zero_halo = jnp.zeros((H, xm_buf.shape[1]), F32)
    if halo:
        xm_buf[0:H, :] = jnp.where(ti > 0, _rms_mod(xp_ref[0], gs, shift), zero_halo)
        xm_buf[H + tile:, :] = jnp.where(ti < n_tiles - 1, _rms_mod(xn_ref[0], gs, shift), zero_halo)
    else:
        xm_buf[0:H, :] = zero_halo
        xm_buf[H + tile:, :] = zero_halo

    ux_buf[...] = jnp.dot(xm_buf[...].astype(BF16), winx_ref[...], preferred_element_type=F32)

    uh = cb_ref[...]
    for k in range(cw_ref.shape[0]):
        off = H - RG_CONV_LEFT + k
        uh = uh + cw_ref[k:k + 1, :] * ux_buf[off:off + tile, :]
    uh_buf[...] = uh

    softplus_neg_lam = jax.nn.softplus(-lam_ref[...])
    for j, k0 in enumerate(gate_starts):
        cols = slice(j * MXU_DIM, (j + 1) * MXU_DIM)
        pre = jnp.dot(uh_buf[:, k0:k0 + GATE_K].astype(BF16), wg_ref[j], preferred_element_type=F32)
        r = jax.nn.sigmoid(pre[:, :MXU_DIM] + ba_ref[:, cols])
        ig = jax.nn.sigmoid(pre[:, MXU_DIM:] + bx_ref[:, cols])
        log_a = (-RG_C) * r * softplus_neg_lam[:, cols]
        a = jnp.exp(log_a)
        a_buf[:, cols] = a
        b_buf[:, cols] = jnp.sqrt(jnp.tanh(-log_a) * (1.0 + a * a)) * (ig * uh_buf[:, cols])

    @pl.when(step == 0)
    def _():
        carry_ref[...] = jnp.broadcast_to(h0_ref[0], carry_ref.shape)

    row = lax.broadcasted_iota(jnp.int32, (SUBLANES, d_rnn), 0)
    n_blocks = tile // SUBLANES

    def scan_block(q, carry):
        blk = (n_blocks - 1 - q) if reverse else q
        r0 = pl.multiple_of(blk * SUBLANES, SUBLANES)
        a = a_buf[pl.ds(r0, SUBLANES), :]
        b = b_buf[pl.ds(r0, SUBLANES), :]
        for d in (1, 2, 4):
            if reverse:
                keep = row < SUBLANES - d
                sh = SUBLANES - d
            else:
                keep = row >= d
                sh = d
            a_s = jnp.where(keep, pltpu.roll(a, sh, 0), 1.0)
            b_s = jnp.where(keep, pltpu.roll(b, sh, 0), 0.0)
            b = a * b_s + b
            a = a * a_s
        h = b + a * carry
        b_buf[pl.ds(r0, SUBLANES), :] = h
        edge = h[0:1, :] if reverse else h[SUBLANES - 1:SUBLANES, :]
        return jnp.broadcast_to(edge, (SUBLANES, d_rnn))

    carry = lax.fori_loop(0, n_blocks, scan_block, carry_ref[...])
    carry_ref[...] = carry

    if mode == "state":
        out_ref[0] = carry[0:1, :]
    elif mode == "store":
        out_ref[0] = b_buf[...]
    else:
        ug = jnp.dot(xm_buf[H:H + tile, :].astype(BF16), wing_ref[...], preferred_element_type=F32)
        v = ((b_buf[...] + hb_ref[0]) * jax.nn.gelu(ug)).astype(BF16)
        y = jnp.dot(v, wout_ref[...], preferred_element_type=F32)
        out_ref[0] = xc_ref[0] + gate * y


def _rglru_call(x, modrows, ng, winx, cw, cb, wg, ba, bx, lam, h0, *, tile, reverse, mode, gate_starts,
                hb=None, wing=None, wout=None):
    bsz, seq, d = x.shape
    d_rnn = winx.shape[1]
    n_tiles = seq // tile
    halo = n_tiles > 1
    hpt = tile // RG_HALO
    n_hblk = seq // RG_HALO

    def t_of(i):
        return (n_tiles - 1 - i) if reverse else i

    in_specs, args = [], []
    if halo:
        in_specs.append(pl.BlockSpec((1, RG_HALO, d), lambda b, i: (b, jnp.maximum(t_of(i) * hpt - 1, 0), 0)))
        args.append(x)
    in_specs.append(pl.BlockSpec((1, tile, d), lambda b, i: (b, t_of(i), 0)))
    args.append(x)
    if halo:
        in_specs.append(
            pl.BlockSpec((1, RG_HALO, d), lambda b, i: (b, jnp.minimum((t_of(i) + 1) * hpt, n_hblk - 1), 0)))
        args.append(x)
    in_specs.append(pl.BlockSpec((1,) + modrows.shape[1:], lambda b, i: (b, 0, 0)))
    args.append(modrows)
    for w in (ng, winx, cw, cb, wg, ba, bx, lam):
        in_specs.append(_const_spec(w.shape))
        args.append(w)
    in_specs.append(pl.BlockSpec((1, 1, d_rnn), lambda b, i: (b, 0, 0)))
    args.append(h0)
    if mode == "out":
        in_specs.append(pl.BlockSpec((1, tile, d_rnn), lambda b, i: (b, t_of(i), 0)))
        args.append(hb)
        for w in (wing, wout):
            in_specs.append(_const_spec(w.shape))
            args.append(w)

    if mode == "state":
        out_spec = pl.BlockSpec((1, 1, d_rnn), lambda b, i: (b, 0, 0))
        out_shape = jax.ShapeDtypeStruct((bsz, 1, d_rnn), F32)
    elif mode == "store":
        out_spec = pl.BlockSpec((1, tile, d_rnn), lambda b, i: (b, t_of(i), 0))
        out_shape = jax.ShapeDtypeStruct((bsz, seq, d_rnn), F32)
    else:
        out_spec = pl.BlockSpec((1, tile, d), lambda b, i: (b, t_of(i), 0))
        out_shape = jax.ShapeDtypeStruct((bsz, seq, d), F32)

    kern = functools.partial(_rglru_kernel, tile=tile, n_tiles=n_tiles, reverse=reverse, mode=mode, halo=halo,
                             gate_starts=gate_starts)
    return pl.pallas_call(
        kern,
        grid=(bsz, n_tiles),
        in_specs=in_specs,
        out_specs=out_spec,
        out_shape=out_shape,
        scratch_shapes=[
            pltpu.VMEM((tile + 2 * RG_HALO, d), F32),
            pltpu.VMEM((tile + 2 * RG_HALO, d_rnn), F32),
            pltpu.VMEM((tile, d_rnn), F32),
            pltpu.VMEM((tile, d_rnn), F32),
            pltpu.VMEM((tile, d_rnn), F32),
            pltpu.VMEM((SUBLANES, d_rnn), F32),
        ],
        compiler_params=pltpu.CompilerParams(
            dimension_semantics=("arbitrary", "arbitrary"), vmem_limit_bytes=VMEM_LIMIT_BYTES),
        name=f"rglru_{mode}_{'bwd' if reverse else 'fwd'}",
    )(*args)


def _shift_rows_down(x):
    r = pltpu.roll(x, 1, 0)
    sub = lax.broadcasted_iota(jnp.int32, (SUBLANES, x.shape[1]), 0)
    return jnp.concatenate([jnp.where(sub == 0, 0.0, r[:SUBLANES]), r[SUBLANES:]], axis=0)


def _shift_rows_up(x):
    n = x.shape[0]
    r = pltpu.roll(x, n - 1, 0)
    sub = lax.broadcasted_iota(jnp.int32, (SUBLANES, x.shape[1]), 0)
    return jnp.concatenate([r[:n - SUBLANES], jnp.where(sub == SUBLANES - 1, 0.0, r[n - SUBLANES:])], axis=0)


def _ffn_kernel(*refs, tile, n_tiles, width, final):
    refs = list(refs)
    hp_ref, hc_ref, hn_ref, mod_ref, ng_ref, wu_ref, cwv_ref, cwg_ref, cbv_ref, cbg_ref, wd_ref = refs[:11]
    refs = refs[11:]
    if final:
        nf_ref = refs[0]
        refs = refs[1:]
    out_ref, xm_buf, u_a, u_b, gated_a, gated_b = refs
    u_slots, gated_slots = (u_a, u_b), (gated_a, gated_b)

    ti = pl.program_id(1)
    shift, scale, gate = mod_ref[0, 3:4, :], mod_ref[0, 4:5, :], mod_ref[0, 5:6, :]
    gs = ng_ref[...] * (1.0 + scale)
    W = width
    n_chunks, cf, _ = wd_ref.shape
    n_grid_rows = tile // W
    slab = FFN_SLAB_GRID_ROWS * W
    assert n_chunks >= 3 and tile % slab == 0 and 2 * W <= slab
    top0, bot0 = tile, tile + W

    zero_halo = jnp.zeros((W, xm_buf.shape[1]), F32)
    xm_buf[0:tile, :] = _rms_mod(hc_ref[0], gs, shift).astype(BF16)
    xm_buf[top0:top0 + W, :] = jnp.where(ti > 0, _rms_mod(hp_ref[0], gs, shift), zero_halo).astype(BF16)
    xm_buf[bot0:bot0 + W, :] = jnp.where(ti < n_tiles - 1, _rms_mod(hn_ref[0], gs, shift), zero_halo).astype(BF16)

    def grid_row_start(r):
        return top0 if r < 0 else (bot0 if r >= n_grid_rows else r * W)

    def up_slab(c, slot, j):
        start = j * slab
        n_rows = min(slab, xm_buf.shape[0] - start)
        y = jnp.dot(xm_buf[start:start + n_rows, :], wu_ref[c], preferred_element_type=F32)
        for which in range(2):
            for k in range(n_rows // W):
                blk = y[k * W:(k + 1) * W, which * cf:(which + 1) * cf]
                r0 = start + k * W
                u_slots[slot][which, 0, r0:r0 + W, :] = _shift_rows_down(blk).astype(BF16)
                u_slots[slot][which, 1, r0:r0 + W, :] = blk.astype(BF16)
                u_slots[slot][which, 2, r0:r0 + W, :] = _shift_rows_up(blk).astype(BF16)

    def conv_rows(u_ref, which, cw, cb, r):
        acc = cb
        for dr in range(3):
            r0 = grid_row_start(r + dr - 1)
            for dc in range(3):
                acc = acc + cw[3 * dr + dc:3 * dr + dc + 1, :] * u_ref[which, dc, r0:r0 + W, :]
        return acc

    def conv_slab(c, slot, i):
        cwv, cwg = cwv_ref[c].astype(BF16), cwg_ref[c].astype(BF16)
        cbv, cbg = cbv_ref[c].astype(BF16), cbg_ref[c].astype(BF16)
        for r in range(i * FFN_SLAB_GRID_ROWS, (i + 1) * FFN_SLAB_GRID_ROWS):
            v = conv_rows(u_slots[slot], 0, cwv, cbv, r)
            g = conv_rows(u_slots[slot], 1, cwg, cbg, r)
            gated_slots[slot][r * W:(r + 1) * W, :] = g * jax.nn.sigmoid(g) * v

    def down_slab(c, slot, i, first):
        sl = slice(i * slab, (i + 1) * slab)
        y = jnp.dot(gated_slots[slot][sl, :], wd_ref[c], preferred_element_type=F32)
        if first:
            out_ref[0, sl, :] = y
        else:
            out_ref[0, sl, :] += y

    n_conv_slabs = tile // slab
    n_up_slabs = n_conv_slabs + 1

    def stage(c_up=None, c_conv=None, c_down=None, first_down=False):
        for i in range(n_up_slabs):
            if c_down is not None and i < n_conv_slabs:
                down_slab(c_down[0], c_down[1], i, first_down)
            if c_up is not None:
                up_slab(c_up[0], c_up[1], i)
            if c_conv is not None and i < n_conv_slabs:
                conv_slab(c_conv[0], c_conv[1], i)

    stage(c_up=(0, 0))
    stage(c_up=(1, 1), c_conv=(0, 0))
    stage(c_up=(2, 0), c_conv=(1, 1), c_down=(0, 0), first_down=True)

    def steady(c, slot):
        stage(c_up=(c + 1, 1 - slot), c_conv=(c, slot), c_down=(c - 1, 1 - slot))

    n_steady = n_chunks - 3

    def steady_pair(p, carry):
        c = 2 + 2 * p
        steady(c, 0)
        steady(c + 1, 1)
        return carry

    lax.fori_loop(0, n_steady // 2, steady_pair, 0)
    if n_steady % 2:
        steady(n_chunks - 2, (n_chunks - 2) % 2)
    last = n_chunks - 1
    stage(c_conv=(last, last % 2), c_down=(last - 1, (last - 1) % 2))
    stage(c_down=(last, last % 2))

    h = hc_ref[0] + gate * out_ref[0]
    if final:
        ms = jnp.mean(h * h, axis=-1, keepdims=True)
        h = h * lax.rsqrt(ms + EPS) * nf_ref[...]
    out_ref[0] = h


def _ffn_call(h, modrows, ng, w_up, conv_w, conv_b, w_down, *, tile, width, norm_final=None):
    bsz, seq, d = h.shape
    d_ff = w_down.shape[0]
    cf = MXU_DIM
    n_chunks = d_ff // cf
    n_tiles = seq // tile
    rpt = tile // width
    n_rows = seq // width

    def chunked_cols(w):
        return w.reshape(w.shape[0], n_chunks, cf).transpose(1, 0, 2)

    cw = conv_w.reshape(9, 2 * d_ff)
    weights = [
        ng,
        jnp.concatenate([chunked_cols(w_up[:, :d_ff]), chunked_cols(w_up[:, d_ff:])], axis=2).astype(BF16),
        chunked_cols(cw[:, :d_ff]), chunked_cols(cw[:, d_ff:]),
        chunked_cols(conv_b[None, :d_ff]), chunked_cols(conv_b[None, d_ff:]),
        w_down.reshape(n_chunks, cf, d).astype(BF16),
    ]
    final = norm_final is not None
    if final:
        weights.append(norm_final)

    in_specs = [
        pl.BlockSpec((1, width, d), lambda b, i: (b, jnp.maximum(i * rpt - 1, 0), 0)),
        pl.BlockSpec((1, tile, d), lambda b, i: (b, i, 0)),
        pl.BlockSpec((1, width, d), lambda b, i: (b, jnp.minimum((i + 1) * rpt, n_rows - 1), 0)),
        pl.BlockSpec((1,) + modrows.shape[1:], lambda b, i: (b, 0, 0)),
    ] + [_const_spec(w.shape) for w in weights]

    rows = tile + 2 * width
    kern = functools.partial(_ffn_kernel, tile=tile, n_tiles=n_tiles, width=width, final=final)
    return pl.pallas_call(
        kern,
        grid=(bsz, n_tiles),
        in_specs=in_specs,
        out_specs=pl.BlockSpec((1, tile, d), lambda b, i: (b, i, 0)),
        out_shape=jax.ShapeDtypeStruct((bsz, seq, d), F32),
        scratch_shapes=[
            pltpu.VMEM((rows, d), BF16),
            pltpu.VMEM((2, 3, rows, cf), BF16),
            pltpu.VMEM((2, 3, rows, cf), BF16),
            pltpu.VMEM((tile, cf), BF16),
            pltpu.VMEM((tile, cf), BF16),
        ],
        compiler_params=pltpu.CompilerParams(
            dimension_semantics=("arbitrary", "arbitrary"), vmem_limit_bytes=VMEM_LIMIT_BYTES),
        name="conv_ffn_final" if final else "conv_ffn",
    )(h, h, h, modrows, *weights)


def _conformer_kernel(hp_ref, hc_ref, hn_ref, mod_ref, ng_ref, w1a_ref, w1g_ref, b1a_ref, b1g_ref, cw_ref, cb_ref,
                      lng_ref, lnb_ref, w2_ref, b2_ref, out_ref, xm_buf, v_buf, y_buf, s_buf, *, tile, n_tiles):
    ti = pl.program_id(1)
    shift, scale, gate = mod_ref[0, 0:1, :], mod_ref[0, 1:2, :], mod_ref[0, 2:3, :]
    gs = ng_ref[...] * (1.0 + scale)
    H = CF_HALO
    d = v_buf.shape[1]

    xm_buf[0:H, :] = _rms_mod(hp_ref[0], gs, shift).astype(BF16)
    xm_buf[H:H + tile, :] = _rms_mod(hc_ref[0], gs, shift).astype(BF16)
    xm_buf[H + tile:, :] = _rms_mod(hn_ref[0], gs, shift).astype(BF16)

    xm = xm_buf[...]
    a = jnp.dot(xm, w1a_ref[...], preferred_element_type=F32) + b1a_ref[...]
    g = jnp.dot(xm, w1g_ref[...], preferred_element_type=F32) + b1g_ref[...]
    v_buf[...] = a * jax.nn.sigmoid(g)

    @pl.when(ti == 0)
    def _():
        v_buf[0:H, :] = jnp.zeros((H, d), F32)

    @pl.when(ti == n_tiles - 1)
    def _():
        v_buf[H + tile:, :] = jnp.zeros((H, d), F32)

    first = H - CF_CONV_LEFT
    n_taps = cw_ref.shape[0]
    RB = CF_ROW_BLOCK
    last_aligned = (first + n_taps - 1) // SUBLANES * SUBLANES
    assert first >= 0 and last_aligned + SUBLANES <= 2 * H and tile % RB == 0

    def row_block(i, carry):
        r0 = pl.multiple_of(i * RB, RB)
        for lb in range(d // MXU_DIM):
            lanes = slice(lb * MXU_DIM, (lb + 1) * MXU_DIM)
            y = None
            for res in range(SUBLANES):
                z = None
                for k in range(n_taps):
                    off = first + k
                    if off % SUBLANES != res:
                        continue
                    term = cw_ref[k:k + 1, lanes] * v_buf[pl.ds(r0 + off - res, RB + SUBLANES), lanes]
                    z = term if z is None else z + term
                if z is not None:
                    zs = z[res:res + RB, :]
                    y = zs if y is None else y + zs
            y_buf[pl.ds(r0, RB), lanes] = y + cb_ref[:, lanes]

        yb = y_buf[pl.ds(r0, RB), :]
        mu = jnp.mean(yb, axis=-1, keepdims=True)
        yc = yb - mu
        var = jnp.mean(yc * yc, axis=-1, keepdims=True)
        yn = yc * lax.rsqrt(var + EPS) * lng_ref[...] + lnb_ref[...]
        s_buf[pl.ds(r0, RB), :] = (yn * jax.nn.sigmoid(yn)).astype(BF16)
        return carry

    lax.fori_loop(0, tile // RB, row_block, 0)

    o = jnp.dot(s_buf[...], w2_ref[...], preferred_element_type=F32) + b2_ref[...]
    out_ref[0] = hc_ref[0] + gate * o


def _conformer_call(h, modrows, ng, w1, b1, cw, cb, lng, lnb, w2, b2, *, tile):
    bsz, seq, d = h.shape
    n_tiles = seq // tile
    hpt = tile // CF_HALO
    n_hblk = seq // CF_HALO
    weights = [ng, w1[:, :d].astype(BF16), w1[:, d:].astype(BF16), b1[None, :d], b1[None, d:], cw, cb[None],
               lng[None], lnb[None], w2.astype(BF16), b2[None]]
    in_specs = [
        pl.BlockSpec((1, CF_HALO, d), lambda b, i: (b, jnp.maximum(i * hpt - 1, 0), 0)),
        pl.BlockSpec((1, tile, d), lambda b, i: (b, i, 0)),
        pl.BlockSpec((1, CF_HALO, d), lambda b, i: (b, jnp.minimum((i + 1) * hpt, n_hblk - 1), 0)),
        pl.BlockSpec((1,) + modrows.shape[1:], lambda b, i: (b, 0, 0)),
    ] + [_const_spec(w.shape) for w in weights]
    kern = functools.partial(_conformer_kernel, tile=tile, n_tiles=n_tiles)
    return pl.pallas_call(
        kern,
        grid=(bsz, n_tiles),
        in_specs=in_specs,
        out_specs=pl.BlockSpec((1, tile, d), lambda b, i: (b, i, 0)),
        out_shape=jax.ShapeDtypeStruct((bsz, seq, d), F32),
        scratch_shapes=[
            pltpu.VMEM((tile + 2 * CF_HALO, d), BF16),
            pltpu.VMEM((tile + 2 * CF_HALO, d), F32),
            pltpu.VMEM((tile, d), F32),
            pltpu.VMEM((tile, d), BF16),
        ],
        compiler_params=pltpu.CompilerParams(
            dimension_semantics=("arbitrary", "arbitrary"), vmem_limit_bytes=VMEM_LIMIT_BYTES),
        name="conformer_conv",
    )(h, h, h, modrows, *weights)


def _tiles(seq):
    return dict(
        rglru=min(512, seq),
        ffn=min(1024, seq),
        conformer=min(512, seq),
    )


def kernel(x, c, ctx, c_ctx, ada_w, ada_b, norm_mix, norm_ffn, rg_w_in, rg_conv_w, rg_conv_b, rg_wa, rg_ba, rg_wx, rg_bx, rg_lam, rg_w_out, cf_w_pw1, cf_b_pw1, cf_conv_w, cf_conv_b, cf_ln_g, cf_ln_b, cf_w_pw2, cf_b_pw2, ffn_w_up, ffn_conv_w, ffn_conv_b, ffn_w_down, norm_final):
    bsz, seq, d = x.shape
    depth = ada_w.shape[0]
    assert depth == 2, "layer 0 = RG-LRU, layer 1 = Conformer conv; the context stream feeds layer 0 only"
    assert seq % GRID_W == 0
    d_rnn = rg_w_in.shape[2] // 2
    head_dim = rg_wa.shape[3]
    tiles = _tiles(seq)
    ctx_len = ctx.shape[1]

    n_mod_rows = -(-(bsz + 1) // SUBLANES) * SUBLANES
    cc = jnp.concatenate([c, c_ctx[None], jnp.zeros((n_mod_rows - bsz - 1, d), F32)], axis=0)
    mod = _modulation(cc, ada_w, ada_b)

    def modrows(layer, rows):
        m = mod[layer][rows].reshape(-1, 6, d)
        return jnp.concatenate([m, jnp.zeros((m.shape[0], SUBLANES - 6, d), F32)], axis=1)

    lat_rows = jnp.arange(bsz)
    ctx_rows = jnp.full((bsz,), bsz)

    k = 0
    gate_starts = _gate_windows(d_rnn, head_dim)
    ng = norm_mix[0][None]
    winx = rg_w_in[k][:, d_rnn:].astype(BF16)
    wing = rg_w_in[k][:, :d_rnn].astype(BF16)
    cw, cb = rg_conv_w[k], rg_conv_b[k][None]
    lam = rg_lam[k]
    zero_state = jnp.zeros((bsz, 1, d_rnn), F32)
    per_dir = []
    for z in range(2):
        per_dir.append(dict(
            wg=_pack_gate_weights(rg_wa[k, z], rg_wx[k, z], gate_starts),
            ba=rg_ba[k, z].reshape(1, d_rnn), bx=rg_bx[k, z].reshape(1, d_rnn), lam=lam[z][None]))

    def rg(inp, mrows, z, h0, mode, tile, **kw):
        p = per_dir[z]
        return _rglru_call(inp, mrows, ng, winx, cw, cb, p["wg"], p["ba"], p["bx"], p["lam"], h0,
                           tile=tile, reverse=bool(z), mode=mode, gate_starts=gate_starts, **kw)

    ctx_mod = modrows(0, ctx_rows)
    lat_mod0 = modrows(0, lat_rows)
    hf0 = rg(ctx, ctx_mod, 0, zero_state, "state", ctx_len)
    hb0 = rg(ctx, ctx_mod, 1, zero_state, "state", ctx_len)
    hb = rg(x, lat_mod0, 1, hb0, "store", tiles["rglru"])
    h = rg(x, lat_mod0, 0, hf0, "out", tiles["rglru"], hb=hb, wing=wing, wout=rg_w_out[k].astype(BF16))

    h = _ffn_call(h, lat_mod0, norm_ffn[0][None], ffn_w_up[0], ffn_conv_w[0], ffn_conv_b[0], ffn_w_down[0],
                  tile=tiles["ffn"], width=GRID_W)

    lat_mod1 = modrows(1, lat_rows)
    h = _conformer_call(h, lat_mod1, norm_mix[1][None], cf_w_pw1[0], cf_b_pw1[0], cf_conv_w[0], cf_conv_b[0],
                        cf_ln_g[0], cf_ln_b[0], cf_w_pw2[0], cf_b_pw2[0], tile=tiles["conformer"])
    h = _ffn_call(h, lat_mod1, norm_ffn[1][None], ffn_w_up[1], ffn_conv_w[1], ffn_conv_b[1], ffn_w_down[1],
                  tile=tiles["ffn"], width=GRID_W, norm_final=norm_final[None])
    return h
```

```python
import functools

import jax
import jax.numpy as jnp
from jax import lax
from jax.experimental import pallas as pl
from jax.experimental.pallas import tpu as pltpu

F32 = jnp.float32
BF16 = jnp.bfloat16

EPS = 1e-6
RG_C = 8.0
GRID_W = 64
RG_CONV_LEFT = 2
CF_CONV_LEFT = 15

SUBLANES = 8
LANES = 128
MXU_DIM = 256
VMEM_LIMIT_BYTES = 56 * 1024 * 1024

RG_HALO = SUBLANES
CF_HALO = 2 * SUBLANES
GATE_K = 2 * MXU_DIM
CF_ROW_BLOCK = 64
BF16_ROWS = 2 * SUBLANES
FFN_SLAB_GRID_ROWS = 4


def _const_spec(shape):
    nd = len(shape)
    return pl.BlockSpec(shape, lambda *_: (0,) * nd, pipeline_mode=pl.Buffered(1))


def _rms_mod(x, gs, shift):
    ms = jnp.mean(x * x, axis=-1, keepdims=True)
    return x * lax.rsqrt(ms + EPS) * gs + shift


def _mod_kernel(cc_ref, w_ref, b_ref, o_ref):
    s = cc_ref[...]
    s = s * jax.nn.sigmoid(s)
    o_ref[0] = jnp.dot(s, w_ref[0], preferred_element_type=F32) + b_ref[0]


def _modulation(cc, ada_w, ada_b):
    depth, d, n = ada_w.shape
    rows = cc.shape[0]
    nc = n // 6
    return pl.pallas_call(
        _mod_kernel,
        grid=(depth, n // nc),
        in_specs=[
            pl.BlockSpec((rows, d), lambda i, j: (0, 0)),
            pl.BlockSpec((1, d, nc), lambda i, j: (i, 0, j)),
            pl.BlockSpec((1, 1, nc), lambda i, j: (i, 0, j)),
        ],
        out_specs=pl.BlockSpec((1, rows, nc), lambda i, j: (i, 0, j)),
        out_shape=jax.ShapeDtypeStruct((depth, rows, n), F32),
        compiler_params=pltpu.CompilerParams(
            dimension_semantics=("arbitrary", "arbitrary"), vmem_limit_bytes=VMEM_LIMIT_BYTES),
        name="adaln_mod",
    )(cc, ada_w, ada_b.reshape(depth, 1, n))


def _gate_windows(d_rnn, head_dim):
    starts = []
    for j in range(d_rnn // MXU_DIM):
        first_row = (j * MXU_DIM // head_dim) * head_dim
        last_row = ((j * MXU_DIM + MXU_DIM - 1) // head_dim + 1) * head_dim
        k0 = min(first_row // LANES * LANES, d_rnn - GATE_K)
        assert k0 <= first_row and last_row <= k0 + GATE_K
        starts.append(k0)
    return tuple(starts)


def _pack_gate_weights(wa, wx, starts):
    dense_a = jax.scipy.linalg.block_diag(*[wa[h] for h in range(wa.shape[0])])
    dense_x = jax.scipy.linalg.block_diag(*[wx[h] for h in range(wx.shape[0])])
    groups = []
    for j, k0 in enumerate(starts):
        cols = slice(j * MXU_DIM, (j + 1) * MXU_DIM)
        groups.append(jnp.concatenate([dense_a[k0:k0 + GATE_K, cols], dense_x[k0:k0 + GATE_K, cols]], axis=1))
    return jnp.stack(groups).astype(BF16)


def _scan_pitch(seg):
    pitch = seg
    while (pitch // SUBLANES) % 2 == 0:
        pitch += SUBLANES
    return pitch


def _rglru_kernel(*refs, tile, n_tiles, reverse, mode, halo, gate_starts):
    refs = list(refs)

    def take(n):
        head = refs[:n]
        del refs[:n]
        return head

    from_x = mode != "out"
    if from_x:
        xp_ref, xc_ref, xn_ref = take(3) if halo else (None, take(1)[0], None)
        mod_ref, ng_ref, winx_ref, cw_ref, cb_ref = take(5)
    else:
        xc_ref, mod_ref, uh_in_ref, ug_in_ref, hb_ref = take(5)
    wg_ref, ba_ref, bx_ref, lam_ref, h0_ref = take(5)
    if mode == "store":
        (wing_ref,) = take(1)
        out_ref, uh_out_ref, ug_out_ref = take(3)
    else:
        if mode == "out":
            (wout_ref,) = take(1)
        (out_ref,) = take(1)
    if from_x:
        xm_buf, ux_buf, uh_buf = take(3)
    a_buf, b_buf, hl_buf, pr_buf, carry_ref = take(5)
    if mode == "out":
        (v_buf,) = take(1)

    step = pl.program_id(1)
    ti = (n_tiles - 1 - step) if reverse else step
    H = RG_HALO
    d_rnn = lam_ref.shape[1]
    n_lane_blocks = d_rnn // LANES
    seg = tile // SUBLANES
    pitch = a_buf.shape[1] // SUBLANES

    if from_x:
        shift, scale = mod_ref[0, 0:1, :], mod_ref[0, 1:2, :]
        gs = ng_ref[...] * (1.0 + scale)
        xm_buf[H:H + tile, :] = _rms_mod(xc_ref[0], gs, shift)
        zero_halo = jnp.zeros((H, xm_buf.shape[1]), F32)
        if halo:
            xm_buf[0:H, :] = jnp.where(ti > 0, _rms_mod(xp_ref[0], gs, shift), zero_halo)
            xm_buf[H + tile:, :] = jnp.where(ti < n_tiles - 1, _rms_mod(xn_ref[0], gs, shift), zero_halo)
        else:
            xm_buf[0:H, :] = zero_halo
            xm_buf[H + tile:, :] = zero_halo

        ux_buf[...] = jnp.dot(xm_buf[...].astype(BF16), winx_ref[...], preferred_element_type=F32)

        n_buf_rows = tile + 2 * H
        for j in range(d_rnn // MXU_DIM):
            cols = slice(j * MXU_DIM, (j + 1) * MXU_DIM)
            ux = ux_buf[:, cols]
            uh = cb_ref[:, cols]
            for k in range(cw_ref.shape[0]):
                off = H - RG_CONV_LEFT + k
                tap = ux[off:off + tile] if off % SUBLANES == 0 else pltpu.roll(ux, n_buf_rows - off, 0)[:tile]
                uh = uh + cw_ref[k:k + 1, cols] * tap
            uh_buf[:, cols] = uh
        uh_src = uh_buf
    else:
        uh_src = uh_in_ref.at[0]

    softplus_neg_lam = jax.nn.softplus(-lam_ref[...])
    for j, k0 in enumerate(gate_starts):
        cols = slice(j * MXU_DIM, (j + 1) * MXU_DIM)
        pre = jnp.dot(uh_src[:, k0:k0 + GATE_K].astype(BF16), wg_ref[j], preferred_element_type=F32)
        r = jax.nn.sigmoid(pre[:, :MXU_DIM] + ba_ref[:, cols])
        ig = jax.nn.sigmoid(pre[:, MXU_DIM:] + bx_ref[:, cols])
        log_a = (-RG_C) * r * softplus_neg_lam[:, cols]
        a = jnp.exp(log_a)
        m2 = jnp.tanh(-log_a) * (1.0 + a * a)
        m = jnp.where(m2 > 0.0, m2 * lax.rsqrt(m2), 0.0)
        b = m * (ig * uh_src[:, cols])
        for half in range(MXU_DIM // LANES):
            lanes = slice(half * LANES, (half + 1) * LANES)
            jb = j * (MXU_DIM // LANES) + half
            for s in range(SUBLANES):
                a_buf[jb, s * pitch:s * pitch + seg, :] = a[s * seg:(s + 1) * seg, lanes]
                b_buf[jb, s * pitch:s * pitch + seg, :] = b[s * seg:(s + 1) * seg, lanes]

    @pl.when(step == 0)
    def _():
        carry_ref[...] = jnp.broadcast_to(h0_ref[0], carry_ref.shape)

    row = lax.broadcasted_iota(jnp.int32, (SUBLANES, LANES), 0)

    def local_step(i, carry):
        q = (seg - 1 - i) if reverse else i
        r0 = pl.multiple_of(q * SUBLANES, SUBLANES)
        hs, prods = [], []
        for jb in range(n_lane_blocks):
            a = a_buf[jb, pl.ds(q, SUBLANES, stride=pitch), :]
            b = b_buf[jb, pl.ds(q, SUBLANES, stride=pitch), :]
            h = a * carry[0][jb] + b
            p = a * carry[1][jb]
            hl_buf[jb, pl.ds(r0, SUBLANES), :] = h
            pr_buf[jb, pl.ds(r0, SUBLANES), :] = p
            hs.append(h)
            prods.append(p)
        return tuple(hs), tuple(prods)

    zeros = tuple(jnp.zeros((SUBLANES, LANES), F32) for _ in range(n_lane_blocks))
    ones = tuple(jnp.ones((SUBLANES, LANES), F32) for _ in range(n_lane_blocks))
    h_end, p_end = lax.fori_loop(0, seg, local_step, (zeros, ones), unroll=2)

    seg_in = []
    for jb in range(n_lane_blocks):
        lanes = slice(jb * LANES, (jb + 1) * LANES)
        a, b = p_end[jb], h_end[jb]
        for d in (1, 2, 4):
            if reverse:
                keep = row < SUBLANES - d
                sh = SUBLANES - d
            else:
                keep = row >= d
                sh = d
            a_s = jnp.where(keep, pltpu.roll(a, sh, 0), 1.0)
            b_s = jnp.where(keep, pltpu.roll(b, sh, 0), 0.0)
            b = a * b_s + b
            a = a * a_s
        carry_in = carry_ref[:, lanes]
        seg_out = b + a * carry_in
        if reverse:
            seg_in.append(jnp.where(row < SUBLANES - 1, pltpu.roll(seg_out, SUBLANES - 1, 0), carry_in))
            edge = seg_out[0:1, :]
        else:
            seg_in.append(jnp.where(row >= 1, pltpu.roll(seg_out, 1, 0), carry_in))
            edge = seg_out[SUBLANES - 1:SUBLANES, :]
        carry_ref[:, lanes] = jnp.broadcast_to(edge, (SUBLANES, LANES))

    def fix_step(q, carry):
        r0 = pl.multiple_of(q * SUBLANES, SUBLANES)
        for jb in range(n_lane_blocks):
            h = hl_buf[jb, pl.ds(r0, SUBLANES), :] + pr_buf[jb, pl.ds(r0, SUBLANES), :] * seg_in[jb]
            b_buf[jb, pl.ds(q, SUBLANES, stride=pitch), :] = h
        return carry

    lax.fori_loop(0, seg, fix_step, 0, unroll=2)

    def states(jb):
        return jnp.concatenate([b_buf[jb, s * pitch:s * pitch + seg, :] for s in range(SUBLANES)], axis=0)

    if mode == "state":
        out_ref[0] = carry_ref[0:1, :]
    elif mode == "store":
        for jb in range(n_lane_blocks):
            out_ref[0, :, jb * LANES:(jb + 1) * LANES] = states(jb)
        uh_out_ref[0] = uh_buf[...]
        ug_out_ref[0] = jnp.dot(xm_buf[H:H + tile, :].astype(BF16), wing_ref[...], preferred_element_type=F32)
    else:
        for jb in range(n_lane_blocks):
            lanes = slice(jb * LANES, (jb + 1) * LANES)
            v_buf[:, lanes] = ((states(jb) + hb_ref[0, :, lanes])
                               * jax.nn.gelu(ug_in_ref[0, :, lanes])).astype(BF16)
        y = jnp.dot(v_buf[...], wout_ref[...], preferred_element_type=F32)
        out_ref[0] = xc_ref[0] + mod_ref[0, 2:3, :] * y


def _rglru_call(x, modrows, gates, h0, *, tile, reverse, mode, gate_starts, proj=None, stored=None, wing=None,
                wout=None):
    bsz, seq, d = x.shape
    d_rnn = gates[3].shape[1]
    n_tiles = seq // tile
    halo = n_tiles > 1
    hpt = tile // RG_HALO
    n_hblk = seq // RG_HALO
    n_lane_blocks = d_rnn // LANES
    assert tile % (SUBLANES * SUBLANES) == 0
    pitch = _scan_pitch(tile // SUBLANES)
    from_x = mode != "out"

    def t_of(i):
        return (n_tiles - 1 - i) if reverse else i

    tile_spec_d = pl.BlockSpec((1, tile, d), lambda b, i: (b, t_of(i), 0))
    tile_spec_r = pl.BlockSpec((1, tile, d_rnn), lambda b, i: (b, t_of(i), 0))
    mod_spec = pl.BlockSpec((1,) + modrows.shape[1:], lambda b, i: (b, 0, 0))

    in_specs, args = [], []
    if from_x:
        if halo:
            in_specs.append(pl.BlockSpec((1, RG_HALO, d), lambda b, i: (b, jnp.maximum(t_of(i) * hpt - 1, 0), 0)))
            args.append(x)
        in_specs.append(tile_spec_d)
        args.append(x)
        if halo:
            in_specs.append(
                pl.BlockSpec((1, RG_HALO, d), lambda b, i: (b, jnp.minimum((t_of(i) + 1) * hpt, n_hblk - 1), 0)))
            args.append(x)
        in_specs.append(mod_spec)
        args.append(modrows)
        consts = list(proj) + list(gates)
    else:
        in_specs += [tile_spec_d, mod_spec, tile_spec_r, tile_spec_r, tile_spec_r]
        args += [x, modrows, *stored]
        consts = list(gates)
    for w in consts:
        in_specs.append(_const_spec(w.shape))
        args.append(w)
    in_specs.append(pl.BlockSpec((1, 1, d_rnn), lambda b, i: (b, 0, 0)))
    args.append(h0)
    for w in ([wing] if mode == "store" else [wout] if mode == "out" else []):
        in_specs.append(_const_spec(w.shape))
        args.append(w)

    if mode == "state":
        out_specs = pl.BlockSpec((1, 1, d_rnn), lambda b, i: (b, 0, 0))
        out_shape = jax.ShapeDtypeStruct((bsz, 1, d_rnn), F32)
    elif mode == "store":
        out_specs = [tile_spec_r] * 3
        out_shape = [jax.ShapeDtypeStruct((bsz, seq, d_rnn), F32)] * 3
    else:
        out_specs = tile_spec_d
        out_shape = jax.ShapeDtypeStruct((bsz, seq, d), F32)

    scratch = []
    if from_x:
        scratch += [
            pltpu.VMEM((tile + 2 * RG_HALO, d), F32),
            pltpu.VMEM((tile + 2 * RG_HALO, d_rnn), F32),
            pltpu.VMEM((tile, d_rnn), F32),
        ]
    scratch += [
        pltpu.VMEM((n_lane_blocks, SUBLANES * pitch, LANES), F32),
        pltpu.VMEM((n_lane_blocks, SUBLANES * pitch, LANES), F32),
        pltpu.VMEM((n_lane_blocks, tile, LANES), F32),
        pltpu.VMEM((n_lane_blocks, tile, LANES), F32),
        pltpu.VMEM((SUBLANES, d_rnn), F32),
    ]
    if mode == "out":
        scratch.append(pltpu.VMEM((tile, d_rnn), BF16))

    kern = functools.partial(_rglru_kernel, tile=tile, n_tiles=n_tiles, reverse=reverse, mode=mode, halo=halo,
                             gate_starts=gate_starts)
    return pl.pallas_call(
        kern,
        grid=(bsz, n_tiles),
        in_specs=in_specs,
        out_specs=out_specs,
        out_shape=out_shape,
        scratch_shapes=scratch,
        compiler_params=pltpu.CompilerParams(
            dimension_semantics=("arbitrary", "arbitrary"), vmem_limit_bytes=VMEM_LIMIT_BYTES),
        name=f"rglru_{mode}_{'bwd' if reverse else 'fwd'}",
    )(*args)


def _shift_rows_down(x):
    r = pltpu.roll(x, 1, 0)
    sub = lax.broadcasted_iota(jnp.int32, (SUBLANES, x.shape[1]), 0)
    return jnp.concatenate([jnp.where(sub == 0, 0.0, r[:SUBLANES]), r[SUBLANES:]], axis=0)


def _shift_rows_up(x):
    n = x.shape[0]
    r = pltpu.roll(x, n - 1, 0)
    sub = lax.broadcasted_iota(jnp.int32, (SUBLANES, x.shape[1]), 0)
    return jnp.concatenate([r[:n - SUBLANES], jnp.where(sub == SUBLANES - 1, 0.0, r[n - SUBLANES:])], axis=0)


def _ffn_kernel(*refs, tile, n_tiles, width, final):
    refs = list(refs)
    hp_ref, hc_ref, hn_ref, mod_ref, ng_ref, wu_ref, cwv_ref, cwg_ref, cbv_ref, cbg_ref, wd_ref = refs[:11]
    refs = refs[11:]
    if final:
        nf_ref = refs[0]
        refs = refs[1:]
    out_ref, xm_buf, u_a, u_b, gated_a, gated_b = refs
    u_slots, gated_slots = (u_a, u_b), (gated_a, gated_b)

    ti = pl.program_id(1)
    shift, scale, gate = mod_ref[0, 3:4, :], mod_ref[0, 4:5, :], mod_ref[0, 5:6, :]
    gs = ng_ref[...] * (1.0 + scale)
    W = width
    n_chunks, cf, _ = wd_ref.shape
    n_grid_rows = tile // W
    slab = FFN_SLAB_GRID_ROWS * W
    assert n_chunks >= 3 and tile % slab == 0 and 2 * W <= slab
    top0, bot0 = tile, tile + W

    zero_halo = jnp.zeros((W, xm_buf.shape[1]), F32)
    xm_buf[0:tile, :] = _rms_mod(hc_ref[0], gs, shift).astype(BF16)
    xm_buf[top0:top0 + W, :] = jnp.where(ti > 0, _rms_mod(hp_ref[0], gs, shift), zero_halo).astype(BF16)
    xm_buf[bot0:bot0 + W, :] = jnp.where(ti < n_tiles - 1, _rms_mod(hn_ref[0], gs, shift), zero_halo).astype(BF16)

    def grid_row_start(r):
        return top0 if r < 0 else (bot0 if r >= n_grid_rows else r * W)

    def up_slab(c, slot, j):
        start = j * slab
        n_rows = min(slab, xm_buf.shape[0] - start)
        y = jnp.dot(xm_buf[start:start + n_rows, :], wu_ref[c], preferred_element_type=F32)
        for which in range(2):
            for k in range(n_rows // W):
                blk = y[k * W:(k + 1) * W, which * cf:(which + 1) * cf]
                r0 = start + k * W
                u_slots[slot][which, 0, r0:r0 + W, :] = _shift_rows_down(blk).astype(BF16)
                u_slots[slot][which, 1, r0:r0 + W, :] = blk.astype(BF16)
                u_slots[slot][which, 2, r0:r0 + W, :] = _shift_rows_up(blk).astype(BF16)

    def conv_rows(u_ref, which, cw, cb, r):
        acc = cb
        for dr in range(3):
            r0 = grid_row_start(r + dr - 1)
            for dc in range(3):
                acc = acc + cw[3 * dr + dc:3 * dr + dc + 1, :] * u_ref[which, dc, r0:r0 + W, :]
        return acc

    def conv_slab(c, slot, i):
        cwv, cwg = cwv_ref[c].astype(BF16), cwg_ref[c].astype(BF16)
        cbv, cbg = cbv_ref[c].astype(BF16), cbg_ref[c].astype(BF16)
        for r in range(i * FFN_SLAB_GRID_ROWS, (i + 1) * FFN_SLAB_GRID_ROWS):
            v = conv_rows(u_slots[slot], 0, cwv, cbv, r)
            g = conv_rows(u_slots[slot], 1, cwg, cbg, r)
            gated_slots[slot][r * W:(r + 1) * W, :] = g * jax.nn.sigmoid(g) * v

    def down_slab(c, slot, i, first):
        sl = slice(i * slab, (i + 1) * slab)
        y = jnp.dot(gated_slots[slot][sl, :], wd_ref[c], preferred_element_type=F32)
        if first:
            out_ref[0, sl, :] = y
        else:
            out_ref[0, sl, :] += y

    n_conv_slabs = tile // slab
    n_up_slabs = n_conv_slabs + 1

    def stage(c_up=None, c_conv=None, c_down=None, first_down=False):
        for i in range(n_up_slabs):
            if c_down is not None and i < n_conv_slabs:
                down_slab(c_down[0], c_down[1], i, first_down)
            if c_up is not None:
                up_slab(c_up[0], c_up[1], i)
            if c_conv is not None and i < n_conv_slabs:
                conv_slab(c_conv[0], c_conv[1], i)

    stage(c_up=(0, 0))
    stage(c_up=(1, 1), c_conv=(0, 0))
    stage(c_up=(2, 0), c_conv=(1, 1), c_down=(0, 0), first_down=True)

    def steady(c, slot):
        stage(c_up=(c + 1, 1 - slot), c_conv=(c, slot), c_down=(c - 1, 1 - slot))

    n_steady = n_chunks - 3

    def steady_pair(p, carry):
        c = 2 + 2 * p
        steady(c, 0)
        steady(c + 1, 1)
        return carry

    lax.fori_loop(0, n_steady // 2, steady_pair, 0)
    if n_steady % 2:
        steady(n_chunks - 2, (n_chunks - 2) % 2)
    last = n_chunks - 1
    stage(c_conv=(last, last % 2), c_down=(last - 1, (last - 1) % 2))
    stage(c_down=(last, last % 2))

    h = hc_ref[0] + gate * out_ref[0]
    if final:
        ms = jnp.mean(h * h, axis=-1, keepdims=True)
        h = h * lax.rsqrt(ms + EPS) * nf_ref[...]
    out_ref[0] = h


def _ffn_call(h, modrows, ng, w_up, conv_w, conv_b, w_down, *, tile, width, norm_final=None):
    bsz, seq, d = h.shape
    d_ff = w_down.shape[0]
    cf = MXU_DIM
    n_chunks = d_ff // cf
    n_tiles = seq // tile
    rpt = tile // width
    n_rows = seq // width

    def chunked_cols(w):
        return w.reshape(w.shape[0], n_chunks, cf).transpose(1, 0, 2)

    cw = conv_w.reshape(9, 2 * d_ff)
    weights = [
        ng,
        jnp.concatenate([chunked_cols(w_up[:, :d_ff]), chunked_cols(w_up[:, d_ff:])], axis=2).astype(BF16),
        chunked_cols(cw[:, :d_ff]), chunked_cols(cw[:, d_ff:]),
        chunked_cols(conv_b[None, :d_ff]), chunked_cols(conv_b[None, d_ff:]),
        w_down.reshape(n_chunks, cf, d).astype(BF16),
    ]
    final = norm_final is not None
    if final:
        weights.append(norm_final)

    in_specs = [
        pl.BlockSpec((1, width, d), lambda b, i: (b, jnp.maximum(i * rpt - 1, 0), 0)),
        pl.BlockSpec((1, tile, d), lambda b, i: (b, i, 0)),
        pl.BlockSpec((1, width, d), lambda b, i: (b, jnp.minimum((i + 1) * rpt, n_rows - 1), 0)),
        pl.BlockSpec((1,) + modrows.shape[1:], lambda b, i: (b, 0, 0)),
    ] + [_const_spec(w.shape) for w in weights]

    rows = tile + 2 * width
    kern = functools.partial(_ffn_kernel, tile=tile, n_tiles=n_tiles, width=width, final=final)
    return pl.pallas_call(
        kern,
        grid=(bsz, n_tiles),
        in_specs=in_specs,
        out_specs=pl.BlockSpec((1, tile, d), lambda b, i: (b, i, 0)),
        out_shape=jax.ShapeDtypeStruct((bsz, seq, d), F32),
        scratch_shapes=[
            pltpu.VMEM((rows, d), BF16),
            pltpu.VMEM((2, 3, rows, cf), BF16),
            pltpu.VMEM((2, 3, rows, cf), BF16),
            pltpu.VMEM((tile, cf), BF16),
            pltpu.VMEM((tile, cf), BF16),
        ],
        compiler_params=pltpu.CompilerParams(
            dimension_semantics=("arbitrary", "arbitrary"), vmem_limit_bytes=VMEM_LIMIT_BYTES),
        name="conv_ffn_final" if final else "conv_ffn",
    )(h, h, h, modrows, *weights)


def _conformer_kernel(hp_ref, hc_ref, hn_ref, mod_ref, ng_ref, w1a_ref, w1g_ref, b1a_ref, b1g_ref, cw_ref, cb_ref,
                      lng_ref, lnb_ref, w2_ref, b2_ref, out_ref, xm_buf, v_buf, y_buf, s_buf, *, tile, n_tiles):
    ti = pl.program_id(1)
    shift, scale, gate = mod_ref[0, 0:1, :], mod_ref[0, 1:2, :], mod_ref[0, 2:3, :]
    gs = ng_ref[...] * (1.0 + scale)
    H = CF_HALO
    d = v_buf.shape[1]

    xm_buf[0:H, :] = _rms_mod(hp_ref[0], gs, shift).astype(BF16)
    xm_buf[H:H + tile, :] = _rms_mod(hc_ref[0], gs, shift).astype(BF16)
    xm_buf[H + tile:, :] = _rms_mod(hn_ref[0], gs, shift).astype(BF16)

    xm = xm_buf[...]
    a = jnp.dot(xm, w1a_ref[...], preferred_element_type=F32) + b1a_ref[...]
    g = jnp.dot(xm, w1g_ref[...], preferred_element_type=F32) + b1g_ref[...]
    v_buf[...] = a * jax.nn.sigmoid(g)

    @pl.when(ti == 0)
    def _():
        v_buf[0:H, :] = jnp.zeros((H, d), F32)

    @pl.when(ti == n_tiles - 1)
    def _():
        v_buf[H + tile:, :] = jnp.zeros((H, d), F32)

    first = H - CF_CONV_LEFT
    n_taps = cw_ref.shape[0]
    RB = CF_ROW_BLOCK
    last_aligned = (first + n_taps - 1) // SUBLANES * SUBLANES
    assert first >= 0 and last_aligned + SUBLANES <= 2 * H and tile % RB == 0

    def row_block(i, carry):
        r0 = pl.multiple_of(i * RB, RB)
        for lb in range(d // MXU_DIM):
            lanes = slice(lb * MXU_DIM, (lb + 1) * MXU_DIM)
            y = None
            for res in range(SUBLANES):
                z = None
                for k in range(n_taps):
                    off = first + k
                    if off % SUBLANES != res:
                        continue
                    term = cw_ref[k:k + 1, lanes] * v_buf[pl.ds(r0 + off - res, RB + SUBLANES), lanes]
                    z = term if z is None else z + term
                if z is not None:
                    zs = z[:RB, :] if res == 0 else pltpu.roll(z, RB + SUBLANES - res, 0)[:RB, :]
                    y = zs if y is None else y + zs
            y_buf[pl.ds(r0, RB), lanes] = y + cb_ref[:, lanes]

        yb = y_buf[pl.ds(r0, RB), :]
        mu = jnp.mean(yb, axis=-1, keepdims=True)
        yc = yb - mu
        var = jnp.mean(yc * yc, axis=-1, keepdims=True)
        yn = yc * lax.rsqrt(var + EPS) * lng_ref[...] + lnb_ref[...]
        s_buf[pl.ds(r0, RB), :] = (yn * jax.nn.sigmoid(yn)).astype(BF16)
        return carry

    lax.fori_loop(0, tile // RB, row_block, 0)

    o = jnp.dot(s_buf[...], w2_ref[...], preferred_element_type=F32) + b2_ref[...]
    out_ref[0] = hc_ref[0] + gate * o


def _conformer_call(h, modrows, ng, w1, b1, cw, cb, lng, lnb, w2, b2, *, tile):
    bsz, seq, d = h.shape
    n_tiles = seq // tile
    hpt = tile // CF_HALO
    n_hblk = seq // CF_HALO
    weights = [ng, w1[:, :d].astype(BF16), w1[:, d:].astype(BF16), b1[None, :d], b1[None, d:], cw, cb[None],
               lng[None], lnb[None], w2.astype(BF16), b2[None]]
    in_specs = [
        pl.BlockSpec((1, CF_HALO, d), lambda b, i: (b, jnp.maximum(i * hpt - 1, 0), 0)),
        pl.BlockSpec((1, tile, d), lambda b, i: (b, i, 0)),
        pl.BlockSpec((1, CF_HALO, d), lambda b, i: (b, jnp.minimum((i + 1) * hpt, n_hblk - 1), 0)),
        pl.BlockSpec((1,) + modrows.shape[1:], lambda b, i: (b, 0, 0)),
    ] + [_const_spec(w.shape) for w in weights]
    kern = functools.partial(_conformer_kernel, tile=tile, n_tiles=n_tiles)
    return pl.pallas_call(
        kern,
        grid=(bsz, n_tiles),
        in_specs=in_specs,
        out_specs=pl.BlockSpec((1, tile, d), lambda b, i: (b, i, 0)),
        out_shape=jax.ShapeDtypeStruct((bsz, seq, d), F32),
        scratch_shapes=[
            pltpu.VMEM((tile + 2 * CF_HALO, d), BF16),
            pltpu.VMEM((tile + 2 * CF_HALO, d), F32),
            pltpu.VMEM((tile, d), F32),
            pltpu.VMEM((tile, d), BF16),
        ],
        compiler_params=pltpu.CompilerParams(
            dimension_semantics=("arbitrary", "arbitrary"), vmem_limit_bytes=VMEM_LIMIT_BYTES),
        name="conformer_conv",
    )(h, h, h, modrows, *weights)


def _tiles(seq):
    return dict(
        rglru=min(512, seq),
        ffn=min(1024, seq),
        conformer=min(512, seq),
    )


def kernel(x, c, ctx, c_ctx, ada_w, ada_b, norm_mix, norm_ffn, rg_w_in, rg_conv_w, rg_conv_b, rg_wa, rg_ba, rg_wx, rg_bx, rg_lam, rg_w_out, cf_w_pw1, cf_b_pw1, cf_conv_w, cf_conv_b, cf_ln_g, cf_ln_b, cf_w_pw2, cf_b_pw2, ffn_w_up, ffn_conv_w, ffn_conv_b, ffn_w_down, norm_final):
    bsz, seq, d = x.shape
    depth = ada_w.shape[0]
    assert depth == 2, "layer 0 = RG-LRU, layer 1 = Conformer conv; the context stream feeds layer 0 only"
    assert seq % GRID_W == 0
    d_rnn = rg_w_in.shape[2] // 2
    head_dim = rg_wa.shape[3]
    tiles = _tiles(seq)
    ctx_len = ctx.shape[1]

    n_mod_rows = -(-(bsz + 1) // SUBLANES) * SUBLANES
    cc = jnp.concatenate([c, c_ctx[None], jnp.zeros((n_mod_rows - bsz - 1, d), F32)], axis=0)
    mod = _modulation(cc, ada_w, ada_b)

    def modrows(layer, rows):
        m = mod[layer][rows].reshape(-1, 6, d)
        return jnp.concatenate([m, jnp.zeros((m.shape[0], SUBLANES - 6, d), F32)], axis=1)

    lat_rows = jnp.arange(bsz)
    ctx_rows = jnp.full((bsz,), bsz)

    k = 0
    gate_starts = _gate_windows(d_rnn, head_dim)
    ng = norm_mix[0][None]
    winx = rg_w_in[k][:, d_rnn:].astype(BF16)
    wing = rg_w_in[k][:, :d_rnn].astype(BF16)
    cw, cb = rg_conv_w[k], rg_conv_b[k][None]
    lam = rg_lam[k]
    zero_state = jnp.zeros((bsz, 1, d_rnn), F32)
    per_dir = []
    for z in range(2):
        per_dir.append(dict(
            wg=_pack_gate_weights(rg_wa[k, z], rg_wx[k, z], gate_starts),
            ba=rg_ba[k, z].reshape(1, d_rnn), bx=rg_bx[k, z].reshape(1, d_rnn), lam=lam[z][None]))

    proj = (ng, winx, cw, cb)

    def rg(inp, mrows, z, h0, mode, tile, **kw):
        p = per_dir[z]
        return _rglru_call(inp, mrows, (p["wg"], p["ba"], p["bx"], p["lam"]), h0,
                           tile=tile, reverse=bool(z), mode=mode, gate_starts=gate_starts, **kw)

    ctx_mod = modrows(0, ctx_rows)
    lat_mod0 = modrows(0, lat_rows)
    hf0 = rg(ctx, ctx_mod, 0, zero_state, "state", ctx_len, proj=proj)
    hb0 = rg(ctx, ctx_mod, 1, zero_state, "state", ctx_len, proj=proj)
    hb, uh, ug = rg(x, lat_mod0, 1, hb0, "store", tiles["rglru"], proj=proj, wing=wing)
    h = rg(x, lat_mod0, 0, hf0, "out", tiles["rglru"], stored=(uh, ug, hb), wout=rg_w_out[k].astype(BF16))

    h = _ffn_call(h, lat_mod0, norm_ffn[0][None], ffn_w_up[0], ffn_conv_w[0], ffn_conv_b[0], ffn_w_down[0],
                  tile=tiles["ffn"], width=GRID_W)

    lat_mod1 = modrows(1, lat_rows)
    h = _conformer_call(h, lat_mod1, norm_mix[1][None], cf_w_pw1[0], cf_b_pw1[0], cf_conv_w[0], cf_conv_b[0],
                        cf_ln_g[0], cf_ln_b[0], cf_w_pw2[0], cf_b_pw2[0], tile=tiles["conformer"])
    h = _ffn_call(h, lat_mod1, norm_ffn[1][None], ffn_w_up[1], ffn_conv_w[1], ffn_conv_b[1], ffn_w_down[1],
                  tile=tiles["ffn"], width=GRID_W, norm_final=norm_final[None])
    return h
```

```python
import functools

import jax
import jax.numpy as jnp
from jax import lax
from jax.experimental import pallas as pl
from jax.experimental.pallas import tpu as pltpu

F32 = jnp.float32
BF16 = jnp.bfloat16

EPS = 1e-6
RG_C = 8.0
GRID_W = 64
RG_CONV_LEFT = 2
CF_CONV_LEFT = 15

SUBLANES = 8
LANES = 128
MXU_DIM = 256
VMEM_LIMIT_BYTES = 56 * 1024 * 1024

RG_HALO = SUBLANES
CF_HALO = 2 * SUBLANES
GATE_K = 2 * MXU_DIM
CF_ROW_BLOCK = 64
BF16_ROWS = 2 * SUBLANES
FFN_SLAB_GRID_ROWS = 4


def _const_spec(shape):
    nd = len(shape)
    return pl.BlockSpec(shape, lambda *_: (0,) * nd, pipeline_mode=pl.Buffered(1))


def _sigmoid(x):
    return 0.5 * jnp.tanh(0.5 * x) + 0.5


def _rms_mod(x, gs, shift):
    ms = jnp.mean(x * x, axis=-1, keepdims=True)
    return x * lax.rsqrt(ms + EPS) * gs + shift


def _mod_kernel(cc_ref, w_ref, b_ref, o_ref):
    s = cc_ref[...]
    s = s * jax.nn.sigmoid(s)
    o_ref[0] = jnp.dot(s, w_ref[0], preferred_element_type=F32) + b_ref[0]


def _modulation(cc, ada_w, ada_b):
    depth, d, n = ada_w.shape
    rows = cc.shape[0]
    nc = n // 6
    return pl.pallas_call(
        _mod_kernel,
        grid=(depth, n // nc),
        in_specs=[
            pl.BlockSpec((rows, d), lambda i, j: (0, 0)),
            pl.BlockSpec((1, d, nc), lambda i, j: (i, 0, j)),
            pl.BlockSpec((1, 1, nc), lambda i, j: (i, 0, j)),
        ],
        out_specs=pl.BlockSpec((1, rows, nc), lambda i, j: (i, 0, j)),
        out_shape=jax.ShapeDtypeStruct((depth, rows, n), F32),
        compiler_params=pltpu.CompilerParams(
            dimension_semantics=("arbitrary", "arbitrary"), vmem_limit_bytes=VMEM_LIMIT_BYTES),
        name="adaln_mod",
    )(cc, ada_w, ada_b.reshape(depth, 1, n))


def _gate_windows(d_rnn, head_dim):
    starts = []
    for j in range(d_rnn // MXU_DIM):
        first_row = (j * MXU_DIM // head_dim) * head_dim
        last_row = ((j * MXU_DIM + MXU_DIM - 1) // head_dim + 1) * head_dim
        k0 = min(first_row // LANES * LANES, d_rnn - GATE_K)
        assert k0 <= first_row and last_row <= k0 + GATE_K
        starts.append(k0)
    return tuple(starts)


def _pack_gate_weights(wa, wx, starts):
    dense_a = jax.scipy.linalg.block_diag(*[wa[h] for h in range(wa.shape[0])])
    dense_x = jax.scipy.linalg.block_diag(*[wx[h] for h in range(wx.shape[0])])
    groups = []
    for j, k0 in enumerate(starts):
        cols = slice(j * MXU_DIM, (j + 1) * MXU_DIM)
        groups.append(jnp.concatenate([dense_a[k0:k0 + GATE_K, cols], dense_x[k0:k0 + GATE_K, cols]], axis=1))
    return jnp.stack(groups).astype(BF16)


def _scan_pitch(seg):
    pitch = seg
    while (pitch // SUBLANES) % 2 == 0:
        pitch += SUBLANES
    return pitch


def _rglru_kernel(*refs, tile, n_tiles, reverse, mode, halo, gate_starts):
    refs = list(refs)

    def take(n):
        head = refs[:n]
        del refs[:n]
        return head

    from_x = mode != "out"
    if from_x:
        xp_ref, xc_ref, xn_ref = take(3) if halo else (None, take(1)[0], None)
        mod_ref, ng_ref, winx_ref, cw_ref, cb_ref = take(5)
    else:
        xc_ref, mod_ref, uh_in_ref, ug_in_ref, hb_ref = take(5)
    wg_ref, ba_ref, bx_ref, lam_ref, h0_ref = take(5)
    if mode == "store":
        (wing_ref,) = take(1)
        out_ref, uh_out_ref, ug_out_ref = take(3)
    else:
        if mode == "out":
            (wout_ref,) = take(1)
        (out_ref,) = take(1)
    if from_x:
        xm_buf, ux_buf, uh_buf = take(3)
    a_buf, b_buf, hl_buf, pr_buf, carry_ref = take(5)
    if mode == "out":
        (v_buf,) = take(1)

    step = pl.program_id(1)
    ti = (n_tiles - 1 - step) if reverse else step
    H = RG_HALO
    d_rnn = lam_ref.shape[1]
    n_lane_blocks = d_rnn // LANES
    seg = tile // SUBLANES
    pitch = a_buf.shape[1] // SUBLANES

    if from_x:
        shift, scale = mod_ref[0, 0:1, :], mod_ref[0, 1:2, :]
        gs = ng_ref[...] * (1.0 + scale)
        xm_buf[H:H + tile, :] = _rms_mod(xc_ref[0], gs, shift)
        zero_halo = jnp.zeros((H, xm_buf.shape[1]), F32)
        if halo:
            xm_buf[0:H, :] = jnp.where(ti > 0, _rms_mod(xp_ref[0], gs, shift), zero_halo)
            xm_buf[H + tile:, :] = jnp.where(ti < n_tiles - 1, _rms_mod(xn_ref[0], gs, shift), zero_halo)
        else:
            xm_buf[0:H, :] = zero_halo
            xm_buf[H + tile:, :] = zero_halo

        ux_buf[...] = jnp.dot(xm_buf[...].astype(BF16), winx_ref[...], preferred_element_type=F32)

        n_buf_rows = tile + 2 * H
        for j in range(d_rnn // MXU_DIM):
            cols = slice(j * MXU_DIM, (j + 1) * MXU_DIM)
            ux = ux_buf[:, cols]
            uh = cb_ref[:, cols]
            for k in range(cw_ref.shape[0]):
                off = H - RG_CONV_LEFT + k
                tap = ux[off:off + tile] if off % SUBLANES == 0 else pltpu.roll(ux, n_buf_rows - off, 0)[:tile]
                uh = uh + cw_ref[k:k + 1, cols] * tap
            uh_buf[:, cols] = uh
        uh_src = uh_buf
    else:
        uh_src = uh_in_ref.at[0]

    neg_c_softplus = (-RG_C) * jax.nn.softplus(-lam_ref[...])
    for j, k0 in enumerate(gate_starts):
        cols = slice(j * MXU_DIM, (j + 1) * MXU_DIM)
        pre = jnp.dot(uh_src[:, k0:k0 + GATE_K].astype(BF16), wg_ref[j], preferred_element_type=F32)
        r = _sigmoid(pre[:, :MXU_DIM] + ba_ref[:, cols])
        ig = _sigmoid(pre[:, MXU_DIM:] + bx_ref[:, cols])
        log_a = r * neg_c_softplus[:, cols]
        a = jnp.exp(log_a)
        m2 = jnp.tanh(-log_a) * (1.0 + a * a)
        m = jnp.where(m2 > 0.0, m2 * lax.rsqrt(m2), 0.0)
        b = m * (ig * uh_src[:, cols])
        for half in range(MXU_DIM // LANES):
            lanes = slice(half * LANES, (half + 1) * LANES)
            jb = j * (MXU_DIM // LANES) + half
            for s in range(SUBLANES):
                a_buf[jb, s * pitch:s * pitch + seg, :] = a[s * seg:(s + 1) * seg, lanes]
                b_buf[jb, s * pitch:s * pitch + seg, :] = b[s * seg:(s + 1) * seg, lanes]

    @pl.when(step == 0)
    def _():
        carry_ref[...] = jnp.broadcast_to(h0_ref[0], carry_ref.shape)

    row = lax.broadcasted_iota(jnp.int32, (SUBLANES, LANES), 0)

    def local_step(i, carry):
        q = (seg - 1 - i) if reverse else i
        r0 = pl.multiple_of(q * SUBLANES, SUBLANES)
        hs, prods = [], []
        for jb in range(n_lane_blocks):
            a = a_buf[jb, pl.ds(q, SUBLANES, stride=pitch), :]
            b = b_buf[jb, pl.ds(q, SUBLANES, stride=pitch), :]
            h = a * carry[0][jb] + b
            p = a * carry[1][jb]
            hl_buf[jb, pl.ds(r0, SUBLANES), :] = h
            pr_buf[jb, pl.ds(r0, SUBLANES), :] = p
            hs.append(h)
            prods.append(p)
        return tuple(hs), tuple(prods)

    zeros = tuple(jnp.zeros((SUBLANES, LANES), F32) for _ in range(n_lane_blocks))
    ones = tuple(jnp.ones((SUBLANES, LANES), F32) for _ in range(n_lane_blocks))
    h_end, p_end = lax.fori_loop(0, seg, local_step, (zeros, ones), unroll=2)

    seg_in = []
    for jb in range(n_lane_blocks):
        lanes = slice(jb * LANES, (jb + 1) * LANES)
        a, b = p_end[jb], h_end[jb]
        for d in (1, 2, 4):
            if reverse:
                keep = row < SUBLANES - d
                sh = SUBLANES - d
            else:
                keep = row >= d
                sh = d
            a_s = jnp.where(keep, pltpu.roll(a, sh, 0), 1.0)
            b_s = jnp.where(keep, pltpu.roll(b, sh, 0), 0.0)
            b = a * b_s + b
            a = a * a_s
        carry_in = carry_ref[:, lanes]
        seg_out = b + a * carry_in
        if reverse:
            seg_in.append(jnp.where(row < SUBLANES - 1, pltpu.roll(seg_out, SUBLANES - 1, 0), carry_in))
            edge = seg_out[0:1, :]
        else:
            seg_in.append(jnp.where(row >= 1, pltpu.roll(seg_out, 1, 0), carry_in))
            edge = seg_out[SUBLANES - 1:SUBLANES, :]
        carry_ref[:, lanes] = jnp.broadcast_to(edge, (SUBLANES, LANES))

    def fix_step(q, carry):
        r0 = pl.multiple_of(q * SUBLANES, SUBLANES)
        for jb in range(n_lane_blocks):
            h = hl_buf[jb, pl.ds(r0, SUBLANES), :] + pr_buf[jb, pl.ds(r0, SUBLANES), :] * seg_in[jb]
            b_buf[jb, pl.ds(q, SUBLANES, stride=pitch), :] = h
        return carry

    lax.fori_loop(0, seg, fix_step, 0, unroll=2)

    def states(jb):
        return jnp.concatenate([b_buf[jb, s * pitch:s * pitch + seg, :] for s in range(SUBLANES)], axis=0)

    if mode == "state":
        out_ref[0] = carry_ref[0:1, :]
    elif mode == "store":
        for jb in range(n_lane_blocks):
            out_ref[0, :, jb * LANES:(jb + 1) * LANES] = states(jb)
        uh_out_ref[0] = uh_buf[...]
        ug_out_ref[0] = jnp.dot(xm_buf[H:H + tile, :].astype(BF16), wing_ref[...], preferred_element_type=F32)
    else:
        for jb in range(n_lane_blocks):
            lanes = slice(jb * LANES, (jb + 1) * LANES)
            v_buf[:, lanes] = ((states(jb) + hb_ref[0, :, lanes])
                               * jax.nn.gelu(ug_in_ref[0, :, lanes])).astype(BF16)
        y = jnp.dot(v_buf[...], wout_ref[...], preferred_element_type=F32)
        out_ref[0] = xc_ref[0] + mod_ref[0, 2:3, :] * y


def _rglru_call(x, modrows, gates, h0, *, tile, reverse, mode, gate_starts, proj=None, stored=None, wing=None,
                wout=None):
    bsz, seq, d = x.shape
    d_rnn = gates[3].shape[1]
    n_tiles = seq // tile
    halo = n_tiles > 1
    hpt = tile // RG_HALO
    n_hblk = seq // RG_HALO
    n_lane_blocks = d_rnn // LANES
    assert tile % (SUBLANES * SUBLANES) == 0
    pitch = _scan_pitch(tile // SUBLANES)
    from_x = mode != "out"

    def t_of(i):
        return (n_tiles - 1 - i) if reverse else i

    tile_spec_d = pl.BlockSpec((1, tile, d), lambda b, i: (b, t_of(i), 0))
    tile_spec_r = pl.BlockSpec((1, tile, d_rnn), lambda b, i: (b, t_of(i), 0))
    mod_spec = pl.BlockSpec((1,) + modrows.shape[1:], lambda b, i: (b, 0, 0))

    in_specs, args = [], []
    if from_x:
        if halo:
            in_specs.append(pl.BlockSpec((1, RG_HALO, d), lambda b, i: (b, jnp.maximum(t_of(i) * hpt - 1, 0), 0)))
            args.append(x)
        in_specs.append(tile_spec_d)
        args.append(x)
        if halo:
            in_specs.append(
                pl.BlockSpec((1, RG_HALO, d), lambda b, i: (b, jnp.minimum((t_of(i) + 1) * hpt, n_hblk - 1), 0)))
            args.append(x)
        in_specs.append(mod_spec)
        args.append(modrows)
        consts = list(proj) + list(gates)
    else:
        in_specs += [tile_spec_d, mod_spec, tile_spec_r, tile_spec_r, tile_spec_r]
        args += [x, modrows, *stored]
        consts = list(gates)
    for w in consts:
        in_specs.append(_const_spec(w.shape))
        args.append(w)
    in_specs.append(pl.BlockSpec((1, 1, d_rnn), lambda b, i: (b, 0, 0)))
    args.append(h0)
    for w in ([wing] if mode == "store" else [wout] if mode == "out" else []):
        in_specs.append(_const_spec(w.shape))
        args.append(w)

    if mode == "state":
        out_specs = pl.BlockSpec((1, 1, d_rnn), lambda b, i: (b, 0, 0))
        out_shape = jax.ShapeDtypeStruct((bsz, 1, d_rnn), F32)
    elif mode == "store":
        out_specs = [tile_spec_r] * 3
        out_shape = [jax.ShapeDtypeStruct((bsz, seq, d_rnn), F32)] * 3
    else:
        out_specs = tile_spec_d
        out_shape = jax.ShapeDtypeStruct((bsz, seq, d), F32)

    scratch = []
    if from_x:
        scratch += [
            pltpu.VMEM((tile + 2 * RG_HALO, d), F32),
            pltpu.VMEM((tile + 2 * RG_HALO, d_rnn), F32),
            pltpu.VMEM((tile, d_rnn), F32),
        ]
    scratch += [
        pltpu.VMEM((n_lane_blocks, SUBLANES * pitch, LANES), F32),
        pltpu.VMEM((n_lane_blocks, SUBLANES * pitch, LANES), F32),
        pltpu.VMEM((n_lane_blocks, tile, LANES), F32),
        pltpu.VMEM((n_lane_blocks, tile, LANES), F32),
        pltpu.VMEM((SUBLANES, d_rnn), F32),
    ]
    if mode == "out":
        scratch.append(pltpu.VMEM((tile, d_rnn), BF16))

    kern = functools.partial(_rglru_kernel, tile=tile, n_tiles=n_tiles, reverse=reverse, mode=mode, halo=halo,
                             gate_starts=gate_starts)
    return pl.pallas_call(
        kern,
        grid=(bsz, n_tiles),
        in_specs=in_specs,
        out_specs=out_specs,
        out_shape=out_shape,
        scratch_shapes=scratch,
        compiler_params=pltpu.CompilerParams(
            dimension_semantics=("arbitrary", "arbitrary"), vmem_limit_bytes=VMEM_LIMIT_BYTES),
        name=f"rglru_{mode}_{'bwd' if reverse else 'fwd'}",
    )(*args)


def _shift_rows_down(x):
    r = pltpu.roll(x, 1, 0)
    sub = lax.broadcasted_iota(jnp.int32, (SUBLANES, x.shape[1]), 0)
    return jnp.concatenate([jnp.where(sub == 0, 0.0, r[:SUBLANES]), r[SUBLANES:]], axis=0)


def _shift_rows_up(x):
    n = x.shape[0]
    r = pltpu.roll(x, n - 1, 0)
    sub = lax.broadcasted_iota(jnp.int32, (SUBLANES, x.shape[1]), 0)
    return jnp.concatenate([r[:n - SUBLANES], jnp.where(sub == SUBLANES - 1, 0.0, r[n - SUBLANES:])], axis=0)


def _ffn_kernel(*refs, tile, n_tiles, width, final):
    refs = list(refs)
    hp_ref, hc_ref, hn_ref, mod_ref, ng_ref, wu_ref, cwv_ref, cwg_ref, cbv_ref, cbg_ref, wd_ref = refs[:11]
    refs = refs[11:]
    if final:
        nf_ref = refs[0]
        refs = refs[1:]
    out_ref, xm_buf, u_a, u_b, gated_a, gated_b = refs
    u_slots, gated_slots = (u_a, u_b), (gated_a, gated_b)

    ti = pl.program_id(1)
    shift, scale, gate = mod_ref[0, 3:4, :], mod_ref[0, 4:5, :], mod_ref[0, 5:6, :]
    gs = ng_ref[...] * (1.0 + scale)
    W = width
    n_chunks, cf, _ = wd_ref.shape
    n_grid_rows = tile // W
    slab = FFN_SLAB_GRID_ROWS * W
    assert n_chunks >= 3 and tile % slab == 0 and 2 * W <= slab
    top0, bot0 = tile, tile + W

    zero_halo = jnp.zeros((W, xm_buf.shape[1]), F32)
    xm_buf[0:tile, :] = _rms_mod(hc_ref[0], gs, shift).astype(BF16)
    xm_buf[top0:top0 + W, :] = jnp.where(ti > 0, _rms_mod(hp_ref[0], gs, shift), zero_halo).astype(BF16)
    xm_buf[bot0:bot0 + W, :] = jnp.where(ti < n_tiles - 1, _rms_mod(hn_ref[0], gs, shift), zero_halo).astype(BF16)

    def grid_row_start(r):
        return top0 if r < 0 else (bot0 if r >= n_grid_rows else r * W)

    def up_slab(c, slot, j):
        start = j * slab
        n_rows = min(slab, xm_buf.shape[0] - start)
        y = jnp.dot(xm_buf[start:start + n_rows, :], wu_ref[c], preferred_element_type=F32)
        for which in range(2):
            for k in range(n_rows // W):
                blk = y[k * W:(k + 1) * W, which * cf:(which + 1) * cf]
                r0 = start + k * W
                u_slots[slot][which, 0, r0:r0 + W, :] = _shift_rows_down(blk).astype(BF16)
                u_slots[slot][which, 1, r0:r0 + W, :] = blk.astype(BF16)
                u_slots[slot][which, 2, r0:r0 + W, :] = _shift_rows_up(blk).astype(BF16)

    def conv_rows(u_ref, which, cw, cb, r):
        acc = cb
        for dr in range(3):
            r0 = grid_row_start(r + dr - 1)
            for dc in range(3):
                acc = acc + cw[3 * dr + dc:3 * dr + dc + 1, :] * u_ref[which, dc, r0:r0 + W, :]
        return acc

    def conv_slab(c, slot, i):
        cwv, cwg = cwv_ref[c].astype(BF16), cwg_ref[c].astype(BF16)
        cbv, cbg = cbv_ref[c].astype(BF16), cbg_ref[c].astype(BF16)
        for r in range(i * FFN_SLAB_GRID_ROWS, (i + 1) * FFN_SLAB_GRID_ROWS):
            v = conv_rows(u_slots[slot], 0, cwv, cbv, r)
            g = conv_rows(u_slots[slot], 1, cwg, cbg, r)
            gated_slots[slot][r * W:(r + 1) * W, :] = g * jax.nn.sigmoid(g) * v

    def down_slab(c, slot, i, first):
        sl = slice(i * slab, (i + 1) * slab)
        y = jnp.dot(gated_slots[slot][sl, :], wd_ref[c], preferred_element_type=F32)
        if first:
            out_ref[0, sl, :] = y
        else:
            out_ref[0, sl, :] += y

    n_conv_slabs = tile // slab
    n_up_slabs = n_conv_slabs + 1

    def stage(c_up=None, c_conv=None, c_down=None, first_down=False):
        for i in range(n_up_slabs):
            if c_down is not None and i < n_conv_slabs:
                down_slab(c_down[0], c_down[1], i, first_down)
            if c_up is not None:
                up_slab(c_up[0], c_up[1], i)
            if c_conv is not None and i < n_conv_slabs:
                conv_slab(c_conv[0], c_conv[1], i)

    stage(c_up=(0, 0))
    stage(c_up=(1, 1), c_conv=(0, 0))
    stage(c_up=(2, 0), c_conv=(1, 1), c_down=(0, 0), first_down=True)

    def steady(c, slot):
        stage(c_up=(c + 1, 1 - slot), c_conv=(c, slot), c_down=(c - 1, 1 - slot))

    n_steady = n_chunks - 3

    def steady_pair(p, carry):
        c = 2 + 2 * p
        steady(c, 0)
        steady(c + 1, 1)
        return carry

    lax.fori_loop(0, n_steady // 2, steady_pair, 0)
    if n_steady % 2:
        steady(n_chunks - 2, (n_chunks - 2) % 2)
    last = n_chunks - 1
    stage(c_conv=(last, last % 2), c_down=(last - 1, (last - 1) % 2))
    stage(c_down=(last, last % 2))

    h = hc_ref[0] + gate * out_ref[0]
    if final:
        ms = jnp.mean(h * h, axis=-1, keepdims=True)
        h = h * lax.rsqrt(ms + EPS) * nf_ref[...]
    out_ref[0] = h


def _ffn_call(h, modrows, ng, w_up, conv_w, conv_b, w_down, *, tile, width, norm_final=None):
    bsz, seq, d = h.shape
    d_ff = w_down.shape[0]
    cf = MXU_DIM
    n_chunks = d_ff // cf
    n_tiles = seq // tile
    rpt = tile // width
    n_rows = seq // width

    def chunked_cols(w):
        return w.reshape(w.shape[0], n_chunks, cf).transpose(1, 0, 2)

    cw = conv_w.reshape(9, 2 * d_ff)
    w_up_b = w_up.astype(BF16)
    weights = [
        ng,
        jnp.concatenate([chunked_cols(w_up_b[:, :d_ff]), chunked_cols(w_up_b[:, d_ff:])], axis=2),
        chunked_cols(cw[:, :d_ff]), chunked_cols(cw[:, d_ff:]),
        chunked_cols(conv_b[None, :d_ff]), chunked_cols(conv_b[None, d_ff:]),
        w_down.reshape(n_chunks, cf, d).astype(BF16),
    ]
    final = norm_final is not None
    if final:
        weights.append(norm_final)

    in_specs = [
        pl.BlockSpec((1, width, d), lambda b, i: (b, jnp.maximum(i * rpt - 1, 0), 0)),
        pl.BlockSpec((1, tile, d), lambda b, i: (b, i, 0)),
        pl.BlockSpec((1, width, d), lambda b, i: (b, jnp.minimum((i + 1) * rpt, n_rows - 1), 0)),
        pl.BlockSpec((1,) + modrows.shape[1:], lambda b, i: (b, 0, 0)),
    ] + [_const_spec(w.shape) for w in weights]

    rows = tile + 2 * width
    kern = functools.partial(_ffn_kernel, tile=tile, n_tiles=n_tiles, width=width, final=final)
    return pl.pallas_call(
        kern,
        grid=(bsz, n_tiles),
        in_specs=in_specs,
        out_specs=pl.BlockSpec((1, tile, d), lambda b, i: (b, i, 0)),
        out_shape=jax.ShapeDtypeStruct((bsz, seq, d), F32),
        scratch_shapes=[
            pltpu.VMEM((rows, d), BF16),
            pltpu.VMEM((2, 3, rows, cf), BF16),
            pltpu.VMEM((2, 3, rows, cf), BF16),
            pltpu.VMEM((tile, cf), BF16),
            pltpu.VMEM((tile, cf), BF16),
        ],
        compiler_params=pltpu.CompilerParams(
            dimension_semantics=("arbitrary", "arbitrary"), vmem_limit_bytes=VMEM_LIMIT_BYTES),
        name="conv_ffn_final" if final else "conv_ffn",
    )(h, h, h, modrows, *weights)


def _conformer_kernel(hp_ref, hc_ref, hn_ref, mod_ref, ng_ref, w1a_ref, w1g_ref, b1a_ref, b1g_ref, cw_ref, cb_ref,
                      lng_ref, lnb_ref, w2_ref, b2_ref, out_ref, xm_buf, v_buf, y_buf, s_buf, *, tile, n_tiles):
    ti = pl.program_id(1)
    shift, scale, gate = mod_ref[0, 0:1, :], mod_ref[0, 1:2, :], mod_ref[0, 2:3, :]
    gs = ng_ref[...] * (1.0 + scale)
    H = CF_HALO
    d = v_buf.shape[1]

    xm_buf[0:H, :] = _rms_mod(hp_ref[0], gs, shift).astype(BF16)
    xm_buf[H:H + tile, :] = _rms_mod(hc_ref[0], gs, shift).astype(BF16)
    xm_buf[H + tile:, :] = _rms_mod(hn_ref[0], gs, shift).astype(BF16)

    xm = xm_buf[...]
    a = jnp.dot(xm, w1a_ref[...], preferred_element_type=F32) + b1a_ref[...]
    g = jnp.dot(xm, w1g_ref[...], preferred_element_type=F32) + b1g_ref[...]
    v_buf[...] = a * _sigmoid(g)

    @pl.when(ti == 0)
    def _():
        v_buf[0:H, :] = jnp.zeros((H, d), F32)

    @pl.when(ti == n_tiles - 1)
    def _():
        v_buf[H + tile:, :] = jnp.zeros((H, d), F32)

    first = H - CF_CONV_LEFT
    n_taps = cw_ref.shape[0]
    RB = CF_ROW_BLOCK
    last_aligned = (first + n_taps - 1) // SUBLANES * SUBLANES
    assert first >= 0 and last_aligned + SUBLANES <= 2 * H and tile % RB == 0

    def row_block(i, carry):
        r0 = pl.multiple_of(i * RB, RB)
        for lb in range(d // MXU_DIM):
            lanes = slice(lb * MXU_DIM, (lb + 1) * MXU_DIM)
            y = None
            for res in range(SUBLANES):
                z = None
                for k in range(n_taps):
                    off = first + k
                    if off % SUBLANES != res:
                        continue
                    term = cw_ref[k:k + 1, lanes] * v_buf[pl.ds(r0 + off - res, RB + SUBLANES), lanes]
                    z = term if z is None else z + term
                if z is not None:
                    zs = z[:RB, :] if res == 0 else pltpu.roll(z, RB + SUBLANES - res, 0)[:RB, :]
                    y = zs if y is None else y + zs
            y_buf[pl.ds(r0, RB), lanes] = y + cb_ref[:, lanes]

        yb = y_buf[pl.ds(r0, RB), :]
        mu = jnp.mean(yb, axis=-1, keepdims=True)
        yc = yb - mu
        var = jnp.mean(yc * yc, axis=-1, keepdims=True)
        yn = yc * lax.rsqrt(var + EPS) * lng_ref[...] + lnb_ref[...]
        s_buf[pl.ds(r0, RB), :] = (yn * _sigmoid(yn)).astype(BF16)
        return carry

    lax.fori_loop(0, tile // RB, row_block, 0)

    o = jnp.dot(s_buf[...], w2_ref[...], preferred_element_type=F32) + b2_ref[...]
    out_ref[0] = hc_ref[0] + gate * o


def _conformer_call(h, modrows, ng, w1, b1, cw, cb, lng, lnb, w2, b2, *, tile):
    bsz, seq, d = h.shape
    n_tiles = seq // tile
    hpt = tile // CF_HALO
    n_hblk = seq // CF_HALO
    weights = [ng, w1[:, :d].astype(BF16), w1[:, d:].astype(BF16), b1[None, :d], b1[None, d:], cw, cb[None],
               lng[None], lnb[None], w2.astype(BF16), b2[None]]
    in_specs = [
        pl.BlockSpec((1, CF_HALO, d), lambda b, i: (b, jnp.maximum(i * hpt - 1, 0), 0)),
        pl.BlockSpec((1, tile, d), lambda b, i: (b, i, 0)),
        pl.BlockSpec((1, CF_HALO, d), lambda b, i: (b, jnp.minimum((i + 1) * hpt, n_hblk - 1), 0)),
        pl.BlockSpec((1,) + modrows.shape[1:], lambda b, i: (b, 0, 0)),
    ] + [_const_spec(w.shape) for w in weights]
    kern = functools.partial(_conformer_kernel, tile=tile, n_tiles=n_tiles)
    return pl.pallas_call(
        kern,
        grid=(bsz, n_tiles),
        in_specs=in_specs,
        out_specs=pl.BlockSpec((1, tile, d), lambda b, i: (b, i, 0)),
        out_shape=jax.ShapeDtypeStruct((bsz, seq, d), F32),
        scratch_shapes=[
            pltpu.VMEM((tile + 2 * CF_HALO, d), BF16),
            pltpu.VMEM((tile + 2 * CF_HALO, d), F32),
            pltpu.VMEM((tile, d), F32),
            pltpu.VMEM((tile, d), BF16),
        ],
        compiler_params=pltpu.CompilerParams(
            dimension_semantics=("arbitrary", "arbitrary"), vmem_limit_bytes=VMEM_LIMIT_BYTES),
        name="conformer_conv",
    )(h, h, h, modrows, *weights)


def _tiles(seq):
    return dict(
        rglru=min(512, seq),
        ffn=min(1024, seq),
        conformer=min(1024, seq),
    )


def kernel(x, c, ctx, c_ctx, ada_w, ada_b, norm_mix, norm_ffn, rg_w_in, rg_conv_w, rg_conv_b, rg_wa, rg_ba, rg_wx, rg_bx, rg_lam, rg_w_out, cf_w_pw1, cf_b_pw1, cf_conv_w, cf_conv_b, cf_ln_g, cf_ln_b, cf_w_pw2, cf_b_pw2, ffn_w_up, ffn_conv_w, ffn_conv_b, ffn_w_down, norm_final):
    bsz, seq, d = x.shape
    depth = ada_w.shape[0]
    assert depth == 2, "layer 0 = RG-LRU, layer 1 = Conformer conv; the context stream feeds layer 0 only"
    assert seq % GRID_W == 0
    d_rnn = rg_w_in.shape[2] // 2
    head_dim = rg_wa.shape[3]
    tiles = _tiles(seq)
    ctx_len = ctx.shape[1]

    n_mod_rows = -(-(bsz + 1) // SUBLANES) * SUBLANES
    cc = jnp.concatenate([c, c_ctx[None], jnp.zeros((n_mod_rows - bsz - 1, d), F32)], axis=0)
    mod = _modulation(cc, ada_w, ada_b)

    def modrows(layer, rows):
        m = mod[layer][rows].reshape(-1, 6, d)
        return jnp.concatenate([m, jnp.zeros((m.shape[0], SUBLANES - 6, d), F32)], axis=1)

    lat_rows = jnp.arange(bsz)
    ctx_rows = jnp.full((bsz,), bsz)

    k = 0
    gate_starts = _gate_windows(d_rnn, head_dim)
    ng = norm_mix[0][None]
    winx = rg_w_in[k][:, d_rnn:].astype(BF16)
    wing = rg_w_in[k][:, :d_rnn].astype(BF16)
    cw, cb = rg_conv_w[k], rg_conv_b[k][None]
    lam = rg_lam[k]
    zero_state = jnp.zeros((bsz, 1, d_rnn), F32)
    per_dir = []
    for z in range(2):
        per_dir.append(dict(
            wg=_pack_gate_weights(rg_wa[k, z], rg_wx[k, z], gate_starts),
            ba=rg_ba[k, z].reshape(1, d_rnn), bx=rg_bx[k, z].reshape(1, d_rnn), lam=lam[z][None]))

    proj = (ng, winx, cw, cb)

    def rg(inp, mrows, z, h0, mode, tile, **kw):
        p = per_dir[z]
        return _rglru_call(inp, mrows, (p["wg"], p["ba"], p["bx"], p["lam"]), h0,
                           tile=tile, reverse=bool(z), mode=mode, gate_starts=gate_starts, **kw)

    ctx_mod = modrows(0, ctx_rows)
    lat_mod0 = modrows(0, lat_rows)
    hf0 = rg(ctx, ctx_mod, 0, zero_state, "state", ctx_len, proj=proj)
    hb0 = rg(ctx, ctx_mod, 1, zero_state, "state", ctx_len, proj=proj)
    hb, uh, ug = rg(x, lat_mod0, 1, hb0, "store", tiles["rglru"], proj=proj, wing=wing)
    h = rg(x, lat_mod0, 0, hf0, "out", tiles["rglru"], stored=(uh, ug, hb), wout=rg_w_out[k].astype(BF16))

    h = _ffn_call(h, lat_mod0, norm_ffn[0][None], ffn_w_up[0], ffn_conv_w[0], ffn_conv_b[0], ffn_w_down[0],
                  tile=tiles["ffn"], width=GRID_W)

    lat_mod1 = modrows(1, lat_rows)
    h = _conformer_call(h, lat_mod1, norm_mix[1][None], cf_w_pw1[0], cf_b_pw1[0], cf_conv_w[0], cf_conv_b[0],
                        cf_ln_g[0], cf_ln_b[0], cf_w_pw2[0], cf_b_pw2[0], tile=tiles["conformer"])
    h = _ffn_call(h, lat_mod1, norm_ffn[1][None], ffn_w_up[1], ffn_conv_w[1], ffn_conv_b[1], ffn_w_down[1],
                  tile=tiles["ffn"], width=GRID_W, norm_final=norm_final[None])
    return h
```

```python
import functools

import jax
import jax.numpy as jnp
from jax import lax
from jax.experimental import pallas as pl
from jax.experimental.pallas import tpu as pltpu

F32 = jnp.float32
BF16 = jnp.bfloat16

EPS = 1e-6
RG_C = 8.0
GRID_W = 64
RG_CONV_LEFT = 2
CF_CONV_LEFT = 15

SUBLANES = 8
LANES = 128
MXU_DIM = 256
VMEM_LIMIT_BYTES = 56 * 1024 * 1024

RG_HALO = SUBLANES
CF_HALO = 2 * SUBLANES
GATE_K = 2 * MXU_DIM
CF_ROW_BLOCK = 64
BF16_ROWS = 2 * SUBLANES
FFN_SLAB_GRID_ROWS = 4


def _const_spec(shape):
    nd = len(shape)
    return pl.BlockSpec(shape, lambda *_: (0,) * nd, pipeline_mode=pl.Buffered(1))


def _sigmoid(x):
    return 0.5 * jnp.tanh(0.5 * x) + 0.5


def _rms_mod(x, gs, shift):
    ms = jnp.mean(x * x, axis=-1, keepdims=True)
    return x * lax.rsqrt(ms + EPS) * gs + shift


def _mod_kernel(cc_ref, w_ref, b_ref, o_ref):
    s = cc_ref[...]
    s = s * jax.nn.sigmoid(s)
    o_ref[0] = jnp.dot(s, w_ref[0], preferred_element_type=F32) + b_ref[0]


def _modulation(cc, ada_w, ada_b):
    depth, d, n = ada_w.shape
    rows = cc.shape[0]
    nc = n // 6
    return pl.pallas_call(
        _mod_kernel,
        grid=(depth, n // nc),
        in_specs=[
            pl.BlockSpec((rows, d), lambda i, j: (0, 0)),
            pl.BlockSpec((1, d, nc), lambda i, j: (i, 0, j)),
            pl.BlockSpec((1, 1, nc), lambda i, j: (i, 0, j)),
        ],
        out_specs=pl.BlockSpec((1, rows, nc), lambda i, j: (i, 0, j)),
        out_shape=jax.ShapeDtypeStruct((depth, rows, n), F32),
        compiler_params=pltpu.CompilerParams(
            dimension_semantics=("arbitrary", "arbitrary"), vmem_limit_bytes=VMEM_LIMIT_BYTES),
        name="adaln_mod",
    )(cc, ada_w, ada_b.reshape(depth, 1, n))


def _gate_windows(d_rnn, head_dim):
    starts = []
    for j in range(d_rnn // MXU_DIM):
        first_row = (j * MXU_DIM // head_dim) * head_dim
        last_row = ((j * MXU_DIM + MXU_DIM - 1) // head_dim + 1) * head_dim
        k0 = min(first_row // LANES * LANES, d_rnn - GATE_K)
        assert k0 <= first_row and last_row <= k0 + GATE_K
        starts.append(k0)
    return tuple(starts)


def _pack_gate_weights(wa, wx, starts):
    dense_a = jax.scipy.linalg.block_diag(*[wa[h] for h in range(wa.shape[0])])
    dense_x = jax.scipy.linalg.block_diag(*[wx[h] for h in range(wx.shape[0])])
    groups = []
    for j, k0 in enumerate(starts):
        cols = slice(j * MXU_DIM, (j + 1) * MXU_DIM)
        groups.append(jnp.concatenate([dense_a[k0:k0 + GATE_K, cols], dense_x[k0:k0 + GATE_K, cols]], axis=1))
    return jnp.stack(groups).astype(BF16)


def _scan_pitch(seg):
    pitch = seg
    while (pitch // SUBLANES) % 2 == 0:
        pitch += SUBLANES
    return pitch


def _rglru_kernel(*refs, tile, n_tiles, reverse, mode, halo, gate_starts):
    refs = list(refs)

    def take(n):
        head = refs[:n]
        del refs[:n]
        return head

    from_x = mode != "out"
    if from_x:
        xp_ref, xc_ref, xn_ref = take(3) if halo else (None, take(1)[0], None)
        mod_ref, ng_ref, winx_ref, cw_ref, cb_ref = take(5)
    else:
        xc_ref, mod_ref, uh_in_ref, ug_in_ref, hb_ref = take(5)
    wg_ref, ba_ref, bx_ref, lam_ref, h0_ref = take(5)
    if mode == "store":
        (wing_ref,) = take(1)
        out_ref, uh_out_ref, ug_out_ref = take(3)
    else:
        if mode == "out":
            (wout_ref,) = take(1)
        (out_ref,) = take(1)
    if from_x:
        xm_buf, ux_buf, uh_buf = take(3)
    a_buf, b_buf, hl_buf, pr_buf, carry_ref = take(5)
    if mode == "out":
        (v_buf,) = take(1)

    step = pl.program_id(1)
    ti = (n_tiles - 1 - step) if reverse else step
    H = RG_HALO
    d_rnn = lam_ref.shape[1]
    n_lane_blocks = d_rnn // LANES
    seg = tile // SUBLANES
    pitch = a_buf.shape[1] // SUBLANES

    if from_x:
        shift, scale = mod_ref[0, 0:1, :], mod_ref[0, 1:2, :]
        gs = ng_ref[...] * (1.0 + scale)
        xm_buf[H:H + tile, :] = _rms_mod(xc_ref[0], gs, shift)
        zero_halo = jnp.zeros((H, xm_buf.shape[1]), F32)
        if halo:
            xm_buf[0:H, :] = jnp.where(ti > 0, _rms_mod(xp_ref[0], gs, shift), zero_halo)
            xm_buf[H + tile:, :] = jnp.where(ti < n_tiles - 1, _rms_mod(xn_ref[0], gs, shift), zero_halo)
        else:
            xm_buf[0:H, :] = zero_halo
            xm_buf[H + tile:, :] = zero_halo

        ux_buf[...] = jnp.dot(xm_buf[...].astype(BF16), winx_ref[...], preferred_element_type=F32)

        n_buf_rows = tile + 2 * H
        for j in range(d_rnn // MXU_DIM):
            cols = slice(j * MXU_DIM, (j + 1) * MXU_DIM)
            ux = ux_buf[:, cols]
            uh = cb_ref[:, cols]
            for k in range(cw_ref.shape[0]):
                off = H - RG_CONV_LEFT + k
                tap = ux[off:off + tile] if off % SUBLANES == 0 else pltpu.roll(ux, n_buf_rows - off, 0)[:tile]
                uh = uh + cw_ref[k:k + 1, cols] * tap
            uh_buf[:, cols] = uh
        uh_src = uh_buf
    else:
        uh_src = uh_in_ref.at[0]

    neg_c_softplus = (-RG_C) * jax.nn.softplus(-lam_ref[...])
    for j, k0 in enumerate(gate_starts):
        cols = slice(j * MXU_DIM, (j + 1) * MXU_DIM)
        pre = jnp.dot(uh_src[:, k0:k0 + GATE_K].astype(BF16), wg_ref[j], preferred_element_type=F32)
        r = _sigmoid(pre[:, :MXU_DIM] + ba_ref[:, cols])
        ig = _sigmoid(pre[:, MXU_DIM:] + bx_ref[:, cols])
        log_a = r * neg_c_softplus[:, cols]
        a = jnp.exp(log_a)
        m2 = jnp.tanh(-log_a) * (1.0 + a * a)
        m = jnp.where(m2 > 0.0, m2 * lax.rsqrt(m2), 0.0)
        b = m * (ig * uh_src[:, cols])
        for half in range(MXU_DIM // LANES):
            lanes = slice(half * LANES, (half + 1) * LANES)
            jb = j * (MXU_DIM // LANES) + half
            for s in range(SUBLANES):
                a_buf[jb, s * pitch:s * pitch + seg, :] = a[s * seg:(s + 1) * seg, lanes]
                b_buf[jb, s * pitch:s * pitch + seg, :] = b[s * seg:(s + 1) * seg, lanes]

    @pl.when(step == 0)
    def _():
        carry_ref[...] = jnp.broadcast_to(h0_ref[0], carry_ref.shape)

    row = lax.broadcasted_iota(jnp.int32, (SUBLANES, LANES), 0)

    def local_step(i, carry):
        q = (seg - 1 - i) if reverse else i
        r0 = pl.multiple_of(q * SUBLANES, SUBLANES)
        hs, prods = [], []
        for jb in range(n_lane_blocks):
            a = a_buf[jb, pl.ds(q, SUBLANES, stride=pitch), :]
            b = b_buf[jb, pl.ds(q, SUBLANES, stride=pitch), :]
            h = a * carry[0][jb] + b
            p = a * carry[1][jb]
            hl_buf[jb, pl.ds(r0, SUBLANES), :] = h
            pr_buf[jb, pl.ds(r0, SUBLANES), :] = p
            hs.append(h)
            prods.append(p)
        return tuple(hs), tuple(prods)

    zeros = tuple(jnp.zeros((SUBLANES, LANES), F32) for _ in range(n_lane_blocks))
    ones = tuple(jnp.ones((SUBLANES, LANES), F32) for _ in range(n_lane_blocks))
    h_end, p_end = lax.fori_loop(0, seg, local_step, (zeros, ones), unroll=2)

    seg_in = []
    for jb in range(n_lane_blocks):
        lanes = slice(jb * LANES, (jb + 1) * LANES)
        a, b = p_end[jb], h_end[jb]
        for d in (1, 2, 4):
            if reverse:
                keep = row < SUBLANES - d
                sh = SUBLANES - d
            else:
                keep = row >= d
                sh = d
            a_s = jnp.where(keep, pltpu.roll(a, sh, 0), 1.0)
            b_s = jnp.where(keep, pltpu.roll(b, sh, 0), 0.0)
            b = a * b_s + b
            a = a * a_s
        carry_in = carry_ref[:, lanes]
        seg_out = b + a * carry_in
        if reverse:
            seg_in.append(jnp.where(row < SUBLANES - 1, pltpu.roll(seg_out, SUBLANES - 1, 0), carry_in))
            edge = seg_out[0:1, :]
        else:
            seg_in.append(jnp.where(row >= 1, pltpu.roll(seg_out, 1, 0), carry_in))
            edge = seg_out[SUBLANES - 1:SUBLANES, :]
        carry_ref[:, lanes] = jnp.broadcast_to(edge, (SUBLANES, LANES))

    def fix_step(q, carry):
        r0 = pl.multiple_of(q * SUBLANES, SUBLANES)
        for jb in range(n_lane_blocks):
            h = hl_buf[jb, pl.ds(r0, SUBLANES), :] + pr_buf[jb, pl.ds(r0, SUBLANES), :] * seg_in[jb]
            b_buf[jb, pl.ds(q, SUBLANES, stride=pitch), :] = h
        return carry

    lax.fori_loop(0, seg, fix_step, 0, unroll=2)

    def states(jb):
        return jnp.concatenate([b_buf[jb, s * pitch:s * pitch + seg, :] for s in range(SUBLANES)], axis=0)

    if mode == "state":
        out_ref[0] = carry_ref[0:1, :]
    elif mode == "store":
        for jb in range(n_lane_blocks):
            out_ref[0, :, jb * LANES:(jb + 1) * LANES] = states(jb)
        uh_out_ref[0] = uh_buf[...]
        ug_out_ref[0] = jnp.dot(xm_buf[H:H + tile, :].astype(BF16), wing_ref[...], preferred_element_type=F32)
    else:
        for jb in range(n_lane_blocks):
            lanes = slice(jb * LANES, (jb + 1) * LANES)
            v_buf[:, lanes] = ((states(jb) + hb_ref[0, :, lanes])
                               * jax.nn.gelu(ug_in_ref[0, :, lanes])).astype(BF16)
        y = jnp.dot(v_buf[...], wout_ref[...], preferred_element_type=F32)
        out_ref[0] = xc_ref[0] + mod_ref[0, 2:3, :] * y


def _rglru_call(x, modrows, gates, h0, *, tile, reverse, mode, gate_starts, proj=None, stored=None, wing=None,
                wout=None):
    bsz, seq, d = x.shape
    d_rnn = gates[3].shape[1]
    n_tiles = seq // tile
    halo = n_tiles > 1
    hpt = tile // RG_HALO
    n_hblk = seq // RG_HALO
    n_lane_blocks = d_rnn // LANES
    assert tile % (SUBLANES * SUBLANES) == 0
    pitch = _scan_pitch(tile // SUBLANES)
    from_x = mode != "out"

    def t_of(i):
        return (n_tiles - 1 - i) if reverse else i

    tile_spec_d = pl.BlockSpec((1, tile, d), lambda b, i: (b, t_of(i), 0))
    tile_spec_r = pl.BlockSpec((1, tile, d_rnn), lambda b, i: (b, t_of(i), 0))
    mod_spec = pl.BlockSpec((1,) + modrows.shape[1:], lambda b, i: (b, 0, 0))

    in_specs, args = [], []
    if from_x:
        if halo:
            in_specs.append(pl.BlockSpec((1, RG_HALO, d), lambda b, i: (b, jnp.maximum(t_of(i) * hpt - 1, 0), 0)))
            args.append(x)
        in_specs.append(tile_spec_d)
        args.append(x)
        if halo:
            in_specs.append(
                pl.BlockSpec((1, RG_HALO, d), lambda b, i: (b, jnp.minimum((t_of(i) + 1) * hpt, n_hblk - 1), 0)))
            args.append(x)
        in_specs.append(mod_spec)
        args.append(modrows)
        consts = list(proj) + list(gates)
    else:
        in_specs += [tile_spec_d, mod_spec, tile_spec_r, tile_spec_r, tile_spec_r]
        args += [x, modrows, *stored]
        consts = list(gates)
    for w in consts:
        in_specs.append(_const_spec(w.shape))
        args.append(w)
    in_specs.append(pl.BlockSpec((1, 1, d_rnn), lambda b, i: (b, 0, 0)))
    args.append(h0)
    for w in ([wing] if mode == "store" else [wout] if mode == "out" else []):
        in_specs.append(_const_spec(w.shape))
        args.append(w)

    if mode == "state":
        out_specs = pl.BlockSpec((1, 1, d_rnn), lambda b, i: (b, 0, 0))
        out_shape = jax.ShapeDtypeStruct((bsz, 1, d_rnn), F32)
    elif mode == "store":
        out_specs = [tile_spec_r] * 3
        out_shape = [jax.ShapeDtypeStruct((bsz, seq, d_rnn), F32)] * 3
    else:
        out_specs = tile_spec_d
        out_shape = jax.ShapeDtypeStruct((bsz, seq, d), F32)

    scratch = []
    if from_x:
        scratch += [
            pltpu.VMEM((tile + 2 * RG_HALO, d), F32),
            pltpu.VMEM((tile + 2 * RG_HALO, d_rnn), F32),
            pltpu.VMEM((tile, d_rnn), F32),
        ]
    scratch += [
        pltpu.VMEM((n_lane_blocks, SUBLANES * pitch, LANES), F32),
        pltpu.VMEM((n_lane_blocks, SUBLANES * pitch, LANES), F32),
        pltpu.VMEM((n_lane_blocks, tile, LANES), F32),
        pltpu.VMEM((n_lane_blocks, tile, LANES), F32),
        pltpu.VMEM((SUBLANES, d_rnn), F32),
    ]
    if mode == "out":
        scratch.append(pltpu.VMEM((tile, d_rnn), BF16))

    kern = functools.partial(_rglru_kernel, tile=tile, n_tiles=n_tiles, reverse=reverse, mode=mode, halo=halo,
                             gate_starts=gate_starts)
    return pl.pallas_call(
        kern,
        grid=(bsz, n_tiles),
        in_specs=in_specs,
        out_specs=out_specs,
        out_shape=out_shape,
        scratch_shapes=scratch,
        compiler_params=pltpu.CompilerParams(
            dimension_semantics=("arbitrary", "arbitrary"), vmem_limit_bytes=VMEM_LIMIT_BYTES),
        name=f"rglru_{mode}_{'bwd' if reverse else 'fwd'}",
    )(*args)


def _shift_rows_down(x):
    r = pltpu.roll(x, 1, 0)
    sub = lax.broadcasted_iota(jnp.int32, (SUBLANES, x.shape[1]), 0)
    return jnp.concatenate([jnp.where(sub == 0, 0.0, r[:SUBLANES]), r[SUBLANES:]], axis=0)


def _shift_rows_up(x):
    n = x.shape[0]
    r = pltpu.roll(x, n - 1, 0)
    sub = lax.broadcasted_iota(jnp.int32, (SUBLANES, x.shape[1]), 0)
    return jnp.concatenate([r[:n - SUBLANES], jnp.where(sub == SUBLANES - 1, 0.0, r[n - SUBLANES:])], axis=0)


def _ffn_kernel(*refs, tile, n_tiles, width, final):
    refs = list(refs)
    hp_ref, hc_ref, hn_ref, mod_ref, ng_ref, wu_ref, cwv_ref, cwg_ref, cbv_ref, cbg_ref, wd_ref = refs[:11]
    refs = refs[11:]
    if final:
        nf_ref = refs[0]
        refs = refs[1:]
    out_ref, xm_buf, u_a, u_b, gated_a, gated_b = refs
    u_slots, gated_slots = (u_a, u_b), (gated_a, gated_b)

    ti = pl.program_id(1)
    shift, scale, gate = mod_ref[0, 3:4, :], mod_ref[0, 4:5, :], mod_ref[0, 5:6, :]
    gs = ng_ref[...] * (1.0 + scale)
    W = width
    n_chunks, cf, _ = wd_ref.shape
    n_grid_rows = tile // W
    slab = FFN_SLAB_GRID_ROWS * W
    assert n_chunks >= 3 and tile % slab == 0 and 2 * W <= slab
    top0, bot0 = tile, tile + W

    zero_halo = jnp.zeros((W, xm_buf.shape[1]), F32)
    xm_buf[0:tile, :] = _rms_mod(hc_ref[0], gs, shift).astype(BF16)
    xm_buf[top0:top0 + W, :] = jnp.where(ti > 0, _rms_mod(hp_ref[0], gs, shift), zero_halo).astype(BF16)
    xm_buf[bot0:bot0 + W, :] = jnp.where(ti < n_tiles - 1, _rms_mod(hn_ref[0], gs, shift), zero_halo).astype(BF16)

    def grid_row_start(r):
        return top0 if r < 0 else (bot0 if r >= n_grid_rows else r * W)

    def up_slab(c, slot, j):
        start = j * slab
        n_rows = min(slab, xm_buf.shape[0] - start)
        y = jnp.dot(xm_buf[start:start + n_rows, :], wu_ref[c], preferred_element_type=F32)
        for which in range(2):
            for k in range(n_rows // W):
                blk = y[k * W:(k + 1) * W, which * cf:(which + 1) * cf]
                r0 = start + k * W
                u_slots[slot][which, 0, r0:r0 + W, :] = _shift_rows_down(blk).astype(BF16)
                u_slots[slot][which, 1, r0:r0 + W, :] = blk.astype(BF16)
                u_slots[slot][which, 2, r0:r0 + W, :] = _shift_rows_up(blk).astype(BF16)

    def conv_rows(u_ref, which, cw, cb, r):
        acc = cb
        for dr in range(3):
            r0 = grid_row_start(r + dr - 1)
            for dc in range(3):
                acc = acc + cw[3 * dr + dc:3 * dr + dc + 1, :] * u_ref[which, dc, r0:r0 + W, :]
        return acc

    def conv_slab(c, i):
        cwv, cwg = cwv_ref[c].astype(BF16), cwg_ref[c].astype(BF16)
        cbv, cbg = cbv_ref[c].astype(BF16), cbg_ref[c].astype(BF16)
        u_ref, gated_ref, half = u_slots[c % 2], gated_slots[(c // 2) % 2], c % 2
        for r in range(i * FFN_SLAB_GRID_ROWS, (i + 1) * FFN_SLAB_GRID_ROWS):
            v = conv_rows(u_ref, 0, cwv, cbv, r)
            g = conv_rows(u_ref, 1, cwg, cbg, r)
            gated_ref[r * W:(r + 1) * W, half * cf:(half + 1) * cf] = g * jax.nn.sigmoid(g) * v

    def down_slab(c0, n, i):
        sl = slice(i * slab, (i + 1) * slab)
        w = wd_ref[c0:c0 + n].reshape(n * cf, wd_ref.shape[2])
        y = jnp.dot(gated_slots[(c0 // 2) % 2][sl, :n * cf], w, preferred_element_type=F32)
        if c0 == 0:
            out_ref[0, sl, :] = y
        else:
            out_ref[0, sl, :] += y

    n_conv_slabs = tile // slab
    n_up_slabs = n_conv_slabs + 1

    for s in range(n_chunks + 2):
        c_up = s if s < n_chunks else None
        c_conv = s - 1 if 1 <= s <= n_chunks else None
        c_down = None
        if s >= 3 and (s - 3) % 2 == 0 and s - 2 < n_chunks:
            c_down = (s - 3, 2)
        elif s == n_chunks + 1 and n_chunks % 2 == 1:
            c_down = (n_chunks - 1, 1)
        for i in range(n_up_slabs):
            if c_down is not None and i < n_conv_slabs:
                down_slab(*c_down, i)
            if c_up is not None:
                up_slab(c_up, c_up % 2, i)
            if c_conv is not None and i < n_conv_slabs:
                conv_slab(c_conv, i)

    h = hc_ref[0] + gate * out_ref[0]
    if final:
        ms = jnp.mean(h * h, axis=-1, keepdims=True)
        h = h * lax.rsqrt(ms + EPS) * nf_ref[...]
    out_ref[0] = h


def _ffn_call(h, modrows, ng, w_up, conv_w, conv_b, w_down, *, tile, width, norm_final=None):
    bsz, seq, d = h.shape
    d_ff = w_down.shape[0]
    cf = MXU_DIM
    n_chunks = d_ff // cf
    n_tiles = seq // tile
    rpt = tile // width
    n_rows = seq // width

    def chunked_cols(w):
        return w.reshape(w.shape[0], n_chunks, cf).transpose(1, 0, 2)

    cw = conv_w.reshape(9, 2 * d_ff)
    w_up_b = w_up.astype(BF16)
    weights = [
        ng,
        jnp.concatenate([chunked_cols(w_up_b[:, :d_ff]), chunked_cols(w_up_b[:, d_ff:])], axis=2),
        chunked_cols(cw[:, :d_ff]), chunked_cols(cw[:, d_ff:]),
        chunked_cols(conv_b[None, :d_ff]), chunked_cols(conv_b[None, d_ff:]),
        w_down.reshape(n_chunks, cf, d).astype(BF16),
    ]
    final = norm_final is not None
    if final:
        weights.append(norm_final)

    in_specs = [
        pl.BlockSpec((1, width, d), lambda b, i: (b, jnp.maximum(i * rpt - 1, 0), 0)),
        pl.BlockSpec((1, tile, d), lambda b, i: (b, i, 0)),
        pl.BlockSpec((1, width, d), lambda b, i: (b, jnp.minimum((i + 1) * rpt, n_rows - 1), 0)),
        pl.BlockSpec((1,) + modrows.shape[1:], lambda b, i: (b, 0, 0)),
    ] + [_const_spec(w.shape) for w in weights]

    rows = tile + 2 * width
    kern = functools.partial(_ffn_kernel, tile=tile, n_tiles=n_tiles, width=width, final=final)
    return pl.pallas_call(
        kern,
        grid=(bsz, n_tiles),
        in_specs=in_specs,
        out_specs=pl.BlockSpec((1, tile, d), lambda b, i: (b, i, 0)),
        out_shape=jax.ShapeDtypeStruct((bsz, seq, d), F32),
        scratch_shapes=[
            pltpu.VMEM((rows, d), BF16),
            pltpu.VMEM((2, 3, rows, cf), BF16),
            pltpu.VMEM((2, 3, rows, cf), BF16),
            pltpu.VMEM((tile, 2 * cf), BF16),
            pltpu.VMEM((tile, 2 * cf), BF16),
        ],
        compiler_params=pltpu.CompilerParams(
            dimension_semantics=("arbitrary", "arbitrary"), vmem_limit_bytes=VMEM_LIMIT_BYTES),
        name="conv_ffn_final" if final else "conv_ffn",
    )(h, h, h, modrows, *weights)


def _conformer_kernel(hp_ref, hc_ref, hn_ref, mod_ref, ng_ref, w1a_ref, w1g_ref, b1a_ref, b1g_ref, cw_ref, cb_ref,
                      lng_ref, lnb_ref, w2_ref, b2_ref, out_ref, xm_buf, v_buf, y_buf, s_buf, *, tile, n_tiles):
    ti = pl.program_id(1)
    shift, scale, gate = mod_ref[0, 0:1, :], mod_ref[0, 1:2, :], mod_ref[0, 2:3, :]
    gs = ng_ref[...] * (1.0 + scale)
    H = CF_HALO
    d = v_buf.shape[1]

    xm_buf[0:H, :] = _rms_mod(hp_ref[0], gs, shift).astype(BF16)
    xm_buf[H:H + tile, :] = _rms_mod(hc_ref[0], gs, shift).astype(BF16)
    xm_buf[H + tile:, :] = _rms_mod(hn_ref[0], gs, shift).astype(BF16)

    xm = xm_buf[...]
    a = jnp.dot(xm, w1a_ref[...], preferred_element_type=F32) + b1a_ref[...]
    g = jnp.dot(xm, w1g_ref[...], preferred_element_type=F32) + b1g_ref[...]
    v_buf[...] = a * _sigmoid(g)

    @pl.when(ti == 0)
    def _():
        v_buf[0:H, :] = jnp.zeros((H, d), F32)

    @pl.when(ti == n_tiles - 1)
    def _():
        v_buf[H + tile:, :] = jnp.zeros((H, d), F32)

    first = H - CF_CONV_LEFT
    n_taps = cw_ref.shape[0]
    RB = CF_ROW_BLOCK
    last_aligned = (first + n_taps - 1) // SUBLANES * SUBLANES
    assert first >= 0 and last_aligned + SUBLANES <= 2 * H and tile % RB == 0

    def row_block(i, carry):
        r0 = pl.multiple_of(i * RB, RB)
        for lb in range(d // MXU_DIM):
            lanes = slice(lb * MXU_DIM, (lb + 1) * MXU_DIM)
            y = None
            for res in range(SUBLANES):
                z = None
                for k in range(n_taps):
                    off = first + k
                    if off % SUBLANES != res:
                        continue
                    term = cw_ref[k:k + 1, lanes] * v_buf[pl.ds(r0 + off - res, RB + SUBLANES), lanes]
                    z = term if z is None else z + term
                if z is not None:
                    zs = z[:RB, :] if res == 0 else pltpu.roll(z, RB + SUBLANES - res, 0)[:RB, :]
                    y = zs if y is None else y + zs
            y_buf[pl.ds(r0, RB), lanes] = y + cb_ref[:, lanes]

        yb = y_buf[pl.ds(r0, RB), :]
        mu = jnp.mean(yb, axis=-1, keepdims=True)
        yc = yb - mu
        var = jnp.mean(yc * yc, axis=-1, keepdims=True)
        yn = yc * lax.rsqrt(var + EPS) * lng_ref[...] + lnb_ref[...]
        s_buf[pl.ds(r0, RB), :] = (yn * _sigmoid(yn)).astype(BF16)
        return carry

    lax.fori_loop(0, tile // RB, row_block, 0)

    o = jnp.dot(s_buf[...], w2_ref[...], preferred_element_type=F32) + b2_ref[...]
    out_ref[0] = hc_ref[0] + gate * o


def _conformer_call(h, modrows, ng, w1, b1, cw, cb, lng, lnb, w2, b2, *, tile):
    bsz, seq, d = h.shape
    n_tiles = seq // tile
    hpt = tile // CF_HALO
    n_hblk = seq // CF_HALO
    weights = [ng, w1[:, :d].astype(BF16), w1[:, d:].astype(BF16), b1[None, :d], b1[None, d:], cw, cb[None],
               lng[None], lnb[None], w2.astype(BF16), b2[None]]
    in_specs = [
        pl.BlockSpec((1, CF_HALO, d), lambda b, i: (b, jnp.maximum(i * hpt - 1, 0), 0)),
        pl.BlockSpec((1, tile, d), lambda b, i: (b, i, 0)),
        pl.BlockSpec((1, CF_HALO, d), lambda b, i: (b, jnp.minimum((i + 1) * hpt, n_hblk - 1), 0)),
        pl.BlockSpec((1,) + modrows.shape[1:], lambda b, i: (b, 0, 0)),
    ] + [_const_spec(w.shape) for w in weights]
    kern = functools.partial(_conformer_kernel, tile=tile, n_tiles=n_tiles)
    return pl.pallas_call(
        kern,
        grid=(bsz, n_tiles),
        in_specs=in_specs,
        out_specs=pl.BlockSpec((1, tile, d), lambda b, i: (b, i, 0)),
        out_shape=jax.ShapeDtypeStruct((bsz, seq, d), F32),
        scratch_shapes=[
            pltpu.VMEM((tile + 2 * CF_HALO, d), BF16),
            pltpu.VMEM((tile + 2 * CF_HALO, d), F32),
            pltpu.VMEM((tile, d), F32),
            pltpu.VMEM((tile, d), BF16),
        ],
        compiler_params=pltpu.CompilerParams(
            dimension_semantics=("arbitrary", "arbitrary"), vmem_limit_bytes=VMEM_LIMIT_BYTES),
        name="conformer_conv",
    )(h, h, h, modrows, *weights)


def _tiles(seq):
    return dict(
        rglru=min(512, seq),
        ffn=min(1024, seq),
        conformer=min(1024, seq),
    )


def kernel(x, c, ctx, c_ctx, ada_w, ada_b, norm_mix, norm_ffn, rg_w_in, rg_conv_w, rg_conv_b, rg_wa, rg_ba, rg_wx, rg_bx, rg_lam, rg_w_out, cf_w_pw1, cf_b_pw1, cf_conv_w, cf_conv_b, cf_ln_g, cf_ln_b, cf_w_pw2, cf_b_pw2, ffn_w_up, ffn_conv_w, ffn_conv_b, ffn_w_down, norm_final):
    bsz, seq, d = x.shape
    depth = ada_w.shape[0]
    assert depth == 2, "layer 0 = RG-LRU, layer 1 = Conformer conv; the context stream feeds layer 0 only"
    assert seq % GRID_W == 0
    d_rnn = rg_w_in.shape[2] // 2
    head_dim = rg_wa.shape[3]
    tiles = _tiles(seq)
    ctx_len = ctx.shape[1]

    n_mod_rows = -(-(bsz + 1) // SUBLANES) * SUBLANES
    cc = jnp.concatenate([c, c_ctx[None], jnp.zeros((n_mod_rows - bsz - 1, d), F32)], axis=0)
    mod = _modulation(cc, ada_w, ada_b)

    def modrows(layer, rows):
        m = mod[layer][rows].reshape(-1, 6, d)
        return jnp.concatenate([m, jnp.zeros((m.shape[0], SUBLANES - 6, d), F32)], axis=1)

    lat_rows = jnp.arange(bsz)
    ctx_rows = jnp.full((bsz,), bsz)

    k = 0
    gate_starts = _gate_windows(d_rnn, head_dim)
    ng = norm_mix[0][None]
    winx = rg_w_in[k][:, d_rnn:].astype(BF16)
    wing = rg_w_in[k][:, :d_rnn].astype(BF16)
    cw, cb = rg_conv_w[k], rg_conv_b[k][None]
    lam = rg_lam[k]
    zero_state = jnp.zeros((bsz, 1, d_rnn), F32)
    per_dir = []
    for z in range(2):
        per_dir.append(dict(
            wg=_pack_gate_weights(rg_wa[k, z], rg_wx[k, z], gate_starts),
            ba=rg_ba[k, z].reshape(1, d_rnn), bx=rg_bx[k, z].reshape(1, d_rnn), lam=lam[z][None]))

    proj = (ng, winx, cw, cb)

    def rg(inp, mrows, z, h0, mode, tile, **kw):
        p = per_dir[z]
        return _rglru_call(inp, mrows, (p["wg"], p["ba"], p["bx"], p["lam"]), h0,
                           tile=tile, reverse=bool(z), mode=mode, gate_starts=gate_starts, **kw)

    ctx_mod = modrows(0, ctx_rows)
    lat_mod0 = modrows(0, lat_rows)
    hf0 = rg(ctx, ctx_mod, 0, zero_state, "state", ctx_len, proj=proj)
    hb0 = rg(ctx, ctx_mod, 1, zero_state, "state", ctx_len, proj=proj)
    hb, uh, ug = rg(x, lat_mod0, 1, hb0, "store", tiles["rglru"], proj=proj, wing=wing)
    h = rg(x, lat_mod0, 0, hf0, "out", tiles["rglru"], stored=(uh, ug, hb), wout=rg_w_out[k].astype(BF16))

    h = _ffn_call(h, lat_mod0, norm_ffn[0][None], ffn_w_up[0], ffn_conv_w[0], ffn_conv_b[0], ffn_w_down[0],
                  tile=tiles["ffn"], width=GRID_W)

    lat_mod1 = modrows(1, lat_rows)
    h = _conformer_call(h, lat_mod1, norm_mix[1][None], cf_w_pw1[0], cf_b_pw1[0], cf_conv_w[0], cf_conv_b[0],
                        cf_ln_g[0], cf_ln_b[0], cf_w_pw2[0], cf_b_pw2[0], tile=tiles["conformer"])
    h = _ffn_call(h, lat_mod1, norm_ffn[1][None], ffn_w_up[1], ffn_conv_w[1], ffn_conv_b[1], ffn_w_down[1],
                  tile=tiles["ffn"], width=GRID_W, norm_final=norm_final[None])
    return h
```

```python
import functools

import jax
import jax.numpy as jnp
from jax import lax
from jax.experimental import pallas as pl
from jax.experimental.pallas import tpu as pltpu

F32 = jnp.float32
BF16 = jnp.bfloat16

EPS = 1e-6
RG_C = 8.0
GRID_W = 64
RG_CONV_LEFT = 2
CF_CONV_LEFT = 15

SUBLANES = 8
LANES = 128
MXU_DIM = 256
VMEM_LIMIT_BYTES = 56 * 1024 * 1024

RG_HALO = SUBLANES
CF_HALO = 2 * SUBLANES
GATE_K = 2 * MXU_DIM
CF_ROW_BLOCK = 64
CF_PW_SLAB = 256
BF16_ROWS = 2 * SUBLANES
FFN_SLAB_GRID_ROWS = 4


def _const_spec(shape):
    nd = len(shape)
    return pl.BlockSpec(shape, lambda *_: (0,) * nd, pipeline_mode=pl.Buffered(1))


def _sigmoid(x):
    return 0.5 * jnp.tanh(0.5 * x) + 0.5


def _rms_mod(x, gs, shift):
    ms = jnp.mean(x * x, axis=-1, keepdims=True)
    return x * lax.rsqrt(ms + EPS) * gs + shift


def _mod_kernel(cc_ref, w_ref, b_ref, o_ref):
    s = cc_ref[...]
    s = s * jax.nn.sigmoid(s)
    o_ref[0] = jnp.dot(s, w_ref[0], preferred_element_type=F32) + b_ref[0]


def _modulation(cc, ada_w, ada_b):
    depth, d, n = ada_w.shape
    rows = cc.shape[0]
    nc = n // 6
    return pl.pallas_call(
        _mod_kernel,
        grid=(depth, n // nc),
        in_specs=[
            pl.BlockSpec((rows, d), lambda i, j: (0, 0)),
            pl.BlockSpec((1, d, nc), lambda i, j: (i, 0, j)),
            pl.BlockSpec((1, 1, nc), lambda i, j: (i, 0, j)),
        ],
        out_specs=pl.BlockSpec((1, rows, nc), lambda i, j: (i, 0, j)),
        out_shape=jax.ShapeDtypeStruct((depth, rows, n), F32),
        compiler_params=pltpu.CompilerParams(
            dimension_semantics=("arbitrary", "arbitrary"), vmem_limit_bytes=VMEM_LIMIT_BYTES),
        name="adaln_mod",
    )(cc, ada_w, ada_b.reshape(depth, 1, n))


def _gate_windows(d_rnn, head_dim):
    starts = []
    for j in range(d_rnn // MXU_DIM):
        first_row = (j * MXU_DIM // head_dim) * head_dim
        last_row = ((j * MXU_DIM + MXU_DIM - 1) // head_dim + 1) * head_dim
        k0 = min(first_row // LANES * LANES, d_rnn - GATE_K)
        assert k0 <= first_row and last_row <= k0 + GATE_K
        starts.append(k0)
    return tuple(starts)


def _pack_gate_weights(wa, wx, starts):
    dense_a = jax.scipy.linalg.block_diag(*[wa[h] for h in range(wa.shape[0])])
    dense_x = jax.scipy.linalg.block_diag(*[wx[h] for h in range(wx.shape[0])])
    groups = []
    for j, k0 in enumerate(starts):
        cols = slice(j * MXU_DIM, (j + 1) * MXU_DIM)
        groups.append(jnp.concatenate([dense_a[k0:k0 + GATE_K, cols], dense_x[k0:k0 + GATE_K, cols]], axis=1))
    return jnp.stack(groups).astype(BF16)


def _scan_pitch(seg):
    pitch = seg
    while (pitch // SUBLANES) % 2 == 0:
        pitch += SUBLANES
    return pitch


def _rglru_kernel(*refs, tile, n_tiles, reverse, mode, halo, gate_starts):
    refs = list(refs)

    def take(n):
        head = refs[:n]
        del refs[:n]
        return head

    from_x = mode != "out"
    if from_x:
        xp_ref, xc_ref, xn_ref = take(3) if halo else (None, take(1)[0], None)
        mod_ref, ng_ref, winx_ref, cw_ref, cb_ref = take(5)
    else:
        xc_ref, mod_ref, uh_in_ref, ug_in_ref, hb_ref = take(5)
    wg_ref, ba_ref, bx_ref, lam_ref, h0_ref = take(5)
    if mode == "store":
        (wing_ref,) = take(1)
        out_ref, uh_out_ref, ug_out_ref = take(3)
    else:
        if mode == "out":
            (wout_ref,) = take(1)
        (out_ref,) = take(1)
    if from_x:
        xm_buf, ux_buf, uh_buf = take(3)
    a_buf, b_buf, hl_buf, pr_buf, carry_ref = take(5)
    if mode == "out":
        (v_buf,) = take(1)

    step = pl.program_id(1)
    ti = (n_tiles - 1 - step) if reverse else step
    H = RG_HALO
    d_rnn = lam_ref.shape[1]
    n_lane_blocks = d_rnn // LANES
    seg = tile // SUBLANES
    pitch = a_buf.shape[1] // SUBLANES

    if from_x:
        shift, scale = mod_ref[0, 0:1, :], mod_ref[0, 1:2, :]
        gs = ng_ref[...] * (1.0 + scale)
        xm_buf[H:H + tile, :] = _rms_mod(xc_ref[0], gs, shift)
        zero_halo = jnp.zeros((H, xm_buf.shape[1]), F32)
        if halo:
            xm_buf[0:H, :] = jnp.where(ti > 0, _rms_mod(xp_ref[0], gs, shift), zero_halo)
            xm_buf[H + tile:, :] = jnp.where(ti < n_tiles - 1, _rms_mod(xn_ref[0], gs, shift), zero_halo)
        else:
            xm_buf[0:H, :] = zero_halo
            xm_buf[H + tile:, :] = zero_halo

        ux_buf[...] = jnp.dot(xm_buf[...].astype(BF16), winx_ref[...], preferred_element_type=F32)

        n_buf_rows = tile + 2 * H
        for j in range(d_rnn // MXU_DIM):
            cols = slice(j * MXU_DIM, (j + 1) * MXU_DIM)
            ux = ux_buf[:, cols]
            uh = cb_ref[:, cols]
            for k in range(cw_ref.shape[0]):
                off = H - RG_CONV_LEFT + k
                tap = ux[off:off + tile] if off % SUBLANES == 0 else pltpu.roll(ux, n_buf_rows - off, 0)[:tile]
                uh = uh + cw_ref[k:k + 1, cols] * tap
            uh_buf[:, cols] = uh
        uh_src = uh_buf
    else:
        uh_src = uh_in_ref.at[0]

    neg_c_softplus = (-RG_C) * jax.nn.softplus(-lam_ref[...])
    for j, k0 in enumerate(gate_starts):
        cols = slice(j * MXU_DIM, (j + 1) * MXU_DIM)
        pre = jnp.dot(uh_src[:, k0:k0 + GATE_K].astype(BF16), wg_ref[j], preferred_element_type=F32)
        r = _sigmoid(pre[:, :MXU_DIM] + ba_ref[:, cols])
        ig = _sigmoid(pre[:, MXU_DIM:] + bx_ref[:, cols])
        log_a = r * neg_c_softplus[:, cols]
        a = jnp.exp(log_a)
        m2 = jnp.tanh(-log_a) * (1.0 + a * a)
        m = jnp.where(m2 > 0.0, m2 * lax.rsqrt(m2), 0.0)
        b = m * (ig * uh_src[:, cols])
        for half in range(MXU_DIM // LANES):
            lanes = slice(half * LANES, (half + 1) * LANES)
            jb = j * (MXU_DIM // LANES) + half
            for s in range(SUBLANES):
                a_buf[jb, s * pitch:s * pitch + seg, :] = a[s * seg:(s + 1) * seg, lanes]
                b_buf[jb, s * pitch:s * pitch + seg, :] = b[s * seg:(s + 1) * seg, lanes]

    @pl.when(step == 0)
    def _():
        carry_ref[...] = jnp.broadcast_to(h0_ref[0], carry_ref.shape)

    row = lax.broadcasted_iota(jnp.int32, (SUBLANES, LANES), 0)

    def local_step(i, carry):
        q = (seg - 1 - i) if reverse else i
        r0 = pl.multiple_of(q * SUBLANES, SUBLANES)
        hs, prods = [], []
        for jb in range(n_lane_blocks):
            a = a_buf[jb, pl.ds(q, SUBLANES, stride=pitch), :]
            b = b_buf[jb, pl.ds(q, SUBLANES, stride=pitch), :]
            h = a * carry[0][jb] + b
            p = a * carry[1][jb]
            hl_buf[jb, pl.ds(r0, SUBLANES), :] = h
            pr_buf[jb, pl.ds(r0, SUBLANES), :] = p
            hs.append(h)
            prods.append(p)
        return tuple(hs), tuple(prods)

    zeros = tuple(jnp.zeros((SUBLANES, LANES), F32) for _ in range(n_lane_blocks))
    ones = tuple(jnp.ones((SUBLANES, LANES), F32) for _ in range(n_lane_blocks))
    h_end, p_end = lax.fori_loop(0, seg, local_step, (zeros, ones), unroll=2)

    seg_in = []
    for jb in range(n_lane_blocks):
        lanes = slice(jb * LANES, (jb + 1) * LANES)
        a, b = p_end[jb], h_end[jb]
        for d in (1, 2, 4):
            if reverse:
                keep = row < SUBLANES - d
                sh = SUBLANES - d
            else:
                keep = row >= d
                sh = d
            a_s = jnp.where(keep, pltpu.roll(a, sh, 0), 1.0)
            b_s = jnp.where(keep, pltpu.roll(b, sh, 0), 0.0)
            b = a * b_s + b
            a = a * a_s
        carry_in = carry_ref[:, lanes]
        seg_out = b + a * carry_in
        if reverse:
            seg_in.append(jnp.where(row < SUBLANES - 1, pltpu.roll(seg_out, SUBLANES - 1, 0), carry_in))
            edge = seg_out[0:1, :]
        else:
            seg_in.append(jnp.where(row >= 1, pltpu.roll(seg_out, 1, 0), carry_in))
            edge = seg_out[SUBLANES - 1:SUBLANES, :]
        carry_ref[:, lanes] = jnp.broadcast_to(edge, (SUBLANES, LANES))

    def fix_step(q, carry):
        r0 = pl.multiple_of(q * SUBLANES, SUBLANES)
        for jb in range(n_lane_blocks):
            h = hl_buf[jb, pl.ds(r0, SUBLANES), :] + pr_buf[jb, pl.ds(r0, SUBLANES), :] * seg_in[jb]
            b_buf[jb, pl.ds(q, SUBLANES, stride=pitch), :] = h
        return carry

    lax.fori_loop(0, seg, fix_step, 0, unroll=2)

    def states(jb):
        return jnp.concatenate([b_buf[jb, s * pitch:s * pitch + seg, :] for s in range(SUBLANES)], axis=0)

    if mode == "state":
        out_ref[0] = carry_ref[0:1, :]
    elif mode == "store":
        for jb in range(n_lane_blocks):
            out_ref[0, :, jb * LANES:(jb + 1) * LANES] = states(jb)
        uh_out_ref[0] = uh_buf[...]
        ug_out_ref[0] = jnp.dot(xm_buf[H:H + tile, :].astype(BF16), wing_ref[...], preferred_element_type=F32)
    else:
        for jb in range(n_lane_blocks):
            lanes = slice(jb * LANES, (jb + 1) * LANES)
            v_buf[:, lanes] = ((states(jb) + hb_ref[0, :, lanes])
                               * jax.nn.gelu(ug_in_ref[0, :, lanes])).astype(BF16)
        y = jnp.dot(v_buf[...], wout_ref[...], preferred_element_type=F32)
        out_ref[0] = xc_ref[0] + mod_ref[0, 2:3, :] * y


def _rglru_call(x, modrows, gates, h0, *, tile, reverse, mode, gate_starts, proj=None, stored=None, wing=None,
                wout=None):
    bsz, seq, d = x.shape
    d_rnn = gates[3].shape[1]
    n_tiles = seq // tile
    halo = n_tiles > 1
    hpt = tile // RG_HALO
    n_hblk = seq // RG_HALO
    n_lane_blocks = d_rnn // LANES
    assert tile % (SUBLANES * SUBLANES) == 0
    pitch = _scan_pitch(tile // SUBLANES)
    from_x = mode != "out"

    def t_of(i):
        return (n_tiles - 1 - i) if reverse else i

    tile_spec_d = pl.BlockSpec((1, tile, d), lambda b, i: (b, t_of(i), 0))
    tile_spec_r = pl.BlockSpec((1, tile, d_rnn), lambda b, i: (b, t_of(i), 0))
    mod_spec = pl.BlockSpec((1,) + modrows.shape[1:], lambda b, i: (b, 0, 0))

    in_specs, args = [], []
    if from_x:
        if halo:
            in_specs.append(pl.BlockSpec((1, RG_HALO, d), lambda b, i: (b, jnp.maximum(t_of(i) * hpt - 1, 0), 0)))
            args.append(x)
        in_specs.append(tile_spec_d)
        args.append(x)
        if halo:
            in_specs.append(
                pl.BlockSpec((1, RG_HALO, d), lambda b, i: (b, jnp.minimum((t_of(i) + 1) * hpt, n_hblk - 1), 0)))
            args.append(x)
        in_specs.append(mod_spec)
        args.append(modrows)
        consts = list(proj) + list(gates)
    else:
        in_specs += [tile_spec_d, mod_spec, tile_spec_r, tile_spec_r, tile_spec_r]
        args += [x, modrows, *stored]
        consts = list(gates)
    for w in consts:
        in_specs.append(_const_spec(w.shape))
        args.append(w)
    in_specs.append(pl.BlockSpec((1, 1, d_rnn), lambda b, i: (b, 0, 0)))
    args.append(h0)
    for w in ([wing] if mode == "store" else [wout] if mode == "out" else []):
        in_specs.append(_const_spec(w.shape))
        args.append(w)

    if mode == "state":
        out_specs = pl.BlockSpec((1, 1, d_rnn), lambda b, i: (b, 0, 0))
        out_shape = jax.ShapeDtypeStruct((bsz, 1, d_rnn), F32)
    elif mode == "store":
        out_specs = [tile_spec_r] * 3
        out_shape = [jax.ShapeDtypeStruct((bsz, seq, d_rnn), F32)] * 3
    else:
        out_specs = tile_spec_d
        out_shape = jax.ShapeDtypeStruct((bsz, seq, d), F32)

    scratch = []
    if from_x:
        scratch += [
            pltpu.VMEM((tile + 2 * RG_HALO, d), F32),
            pltpu.VMEM((tile + 2 * RG_HALO, d_rnn), F32),
            pltpu.VMEM((tile, d_rnn), F32),
        ]
    scratch += [
        pltpu.VMEM((n_lane_blocks, SUBLANES * pitch, LANES), F32),
        pltpu.VMEM((n_lane_blocks, SUBLANES * pitch, LANES), F32),
        pltpu.VMEM((n_lane_blocks, tile, LANES), F32),
        pltpu.VMEM((n_lane_blocks, tile, LANES), F32),
        pltpu.VMEM((SUBLANES, d_rnn), F32),
    ]
    if mode == "out":
        scratch.append(pltpu.VMEM((tile, d_rnn), BF16))

    kern = functools.partial(_rglru_kernel, tile=tile, n_tiles=n_tiles, reverse=reverse, mode=mode, halo=halo,
                             gate_starts=gate_starts)
    return pl.pallas_call(
        kern,
        grid=(bsz, n_tiles),
        in_specs=in_specs,
        out_specs=out_specs,
        out_shape=out_shape,
        scratch_shapes=scratch,
        compiler_params=pltpu.CompilerParams(
            dimension_semantics=("arbitrary", "arbitrary"), vmem_limit_bytes=VMEM_LIMIT_BYTES),
        name=f"rglru_{mode}_{'bwd' if reverse else 'fwd'}",
    )(*args)


def _shift_rows_down(x):
    r = pltpu.roll(x, 1, 0)
    sub = lax.broadcasted_iota(jnp.int32, (SUBLANES, x.shape[1]), 0)
    return jnp.concatenate([jnp.where(sub == 0, 0.0, r[:SUBLANES]), r[SUBLANES:]], axis=0)


def _shift_rows_up(x):
    n = x.shape[0]
    r = pltpu.roll(x, n - 1, 0)
    sub = lax.broadcasted_iota(jnp.int32, (SUBLANES, x.shape[1]), 0)
    return jnp.concatenate([r[:n - SUBLANES], jnp.where(sub == SUBLANES - 1, 0.0, r[n - SUBLANES:])], axis=0)


def _ffn_kernel(*refs, tile, n_tiles, width, final):
    refs = list(refs)
    hp_ref, hc_ref, hn_ref, mod_ref, ng_ref, wu_ref, cwv_ref, cwg_ref, cbv_ref, cbg_ref, wd_ref = refs[:11]
    refs = refs[11:]
    if final:
        nf_ref = refs[0]
        refs = refs[1:]
    out_ref, xm_buf, u_a, u_b, gated_a, gated_b = refs
    u_slots, gated_slots = (u_a, u_b), (gated_a, gated_b)

    ti = pl.program_id(1)
    shift, scale, gate = mod_ref[0, 3:4, :], mod_ref[0, 4:5, :], mod_ref[0, 5:6, :]
    gs = ng_ref[...] * (1.0 + scale)
    W = width
    n_chunks, cf, _ = wd_ref.shape
    n_grid_rows = tile // W
    slab = FFN_SLAB_GRID_ROWS * W
    assert n_chunks >= 3 and tile % slab == 0 and 2 * W <= slab
    top0, bot0 = tile, tile + W

    zero_halo = jnp.zeros((W, xm_buf.shape[1]), F32)
    xm_buf[0:tile, :] = _rms_mod(hc_ref[0], gs, shift).astype(BF16)
    xm_buf[top0:top0 + W, :] = jnp.where(ti > 0, _rms_mod(hp_ref[0], gs, shift), zero_halo).astype(BF16)
    xm_buf[bot0:bot0 + W, :] = jnp.where(ti < n_tiles - 1, _rms_mod(hn_ref[0], gs, shift), zero_halo).astype(BF16)

    def grid_row_start(r):
        return top0 if r < 0 else (bot0 if r >= n_grid_rows else r * W)

    def up_slab(c, slot, j):
        start = j * slab
        n_rows = min(slab, xm_buf.shape[0] - start)
        y = jnp.dot(xm_buf[start:start + n_rows, :], wu_ref[c], preferred_element_type=F32)
        for which in range(2):
            for k in range(n_rows // W):
                blk = y[k * W:(k + 1) * W, which * cf:(which + 1) * cf]
                r0 = start + k * W
                u_slots[slot][which, 0, r0:r0 + W, :] = _shift_rows_down(blk).astype(BF16)
                u_slots[slot][which, 1, r0:r0 + W, :] = blk.astype(BF16)
                u_slots[slot][which, 2, r0:r0 + W, :] = _shift_rows_up(blk).astype(BF16)

    def conv_rows(u_ref, which, cw, cb, r):
        acc = cb
        for dr in range(3):
            r0 = grid_row_start(r + dr - 1)
            for dc in range(3):
                acc = acc + cw[3 * dr + dc:3 * dr + dc + 1, :] * u_ref[which, dc, r0:r0 + W, :]
        return acc

    def conv_slab(c, i):
        cwv, cwg = cwv_ref[c].astype(BF16), cwg_ref[c].astype(BF16)
        cbv, cbg = cbv_ref[c].astype(BF16), cbg_ref[c].astype(BF16)
        u_ref, gated_ref, half = u_slots[c % 2], gated_slots[(c // 2) % 2], c % 2
        for r in range(i * FFN_SLAB_GRID_ROWS, (i + 1) * FFN_SLAB_GRID_ROWS):
            v = conv_rows(u_ref, 0, cwv, cbv, r)
            g = conv_rows(u_ref, 1, cwg, cbg, r)
            gated_ref[r * W:(r + 1) * W, half * cf:(half + 1) * cf] = g * jax.nn.sigmoid(g) * v

    def down_slab(c0, n, i):
        sl = slice(i * slab, (i + 1) * slab)
        w = wd_ref[c0:c0 + n].reshape(n * cf, wd_ref.shape[2])
        y = jnp.dot(gated_slots[(c0 // 2) % 2][sl, :n * cf], w, preferred_element_type=F32)
        if c0 == 0:
            out_ref[0, sl, :] = y
        else:
            out_ref[0, sl, :] += y

    n_conv_slabs = tile // slab
    n_up_slabs = n_conv_slabs + 1

    for s in range(n_chunks + 2):
        c_up = s if s < n_chunks else None
        c_conv = s - 1 if 1 <= s <= n_chunks else None
        c_down = None
        if s >= 3 and (s - 3) % 2 == 0 and s - 2 < n_chunks:
            c_down = (s - 3, 2)
        elif s == n_chunks + 1 and n_chunks % 2 == 1:
            c_down = (n_chunks - 1, 1)
        for i in range(n_up_slabs):
            if c_down is not None and i < n_conv_slabs:
                down_slab(*c_down, i)
            if c_up is not None:
                up_slab(c_up, c_up % 2, i)
            if c_conv is not None and i < n_conv_slabs:
                conv_slab(c_conv, i)

    h = hc_ref[0] + gate * out_ref[0]
    if final:
        ms = jnp.mean(h * h, axis=-1, keepdims=True)
        h = h * lax.rsqrt(ms + EPS) * nf_ref[...]
    out_ref[0] = h


def _ffn_call(h, modrows, ng, w_up, conv_w, conv_b, w_down, *, tile, width, norm_final=None):
    bsz, seq, d = h.shape
    d_ff = w_down.shape[0]
    cf = MXU_DIM
    n_chunks = d_ff // cf
    n_tiles = seq // tile
    rpt = tile // width
    n_rows = seq // width

    def chunked_cols(w):
        return w.reshape(w.shape[0], n_chunks, cf).transpose(1, 0, 2)

    cw = conv_w.reshape(9, 2 * d_ff)
    w_up_b = w_up.astype(BF16)
    weights = [
        ng,
        jnp.concatenate([chunked_cols(w_up_b[:, :d_ff]), chunked_cols(w_up_b[:, d_ff:])], axis=2),
        chunked_cols(cw[:, :d_ff]), chunked_cols(cw[:, d_ff:]),
        chunked_cols(conv_b[None, :d_ff]), chunked_cols(conv_b[None, d_ff:]),
        w_down.reshape(n_chunks, cf, d).astype(BF16),
    ]
    final = norm_final is not None
    if final:
        weights.append(norm_final)

    in_specs = [
        pl.BlockSpec((1, width, d), lambda b, i: (b, jnp.maximum(i * rpt - 1, 0), 0)),
        pl.BlockSpec((1, tile, d), lambda b, i: (b, i, 0)),
        pl.BlockSpec((1, width, d), lambda b, i: (b, jnp.minimum((i + 1) * rpt, n_rows - 1), 0)),
        pl.BlockSpec((1,) + modrows.shape[1:], lambda b, i: (b, 0, 0)),
    ] + [_const_spec(w.shape) for w in weights]

    rows = tile + 2 * width
    kern = functools.partial(_ffn_kernel, tile=tile, n_tiles=n_tiles, width=width, final=final)
    return pl.pallas_call(
        kern,
        grid=(bsz, n_tiles),
        in_specs=in_specs,
        out_specs=pl.BlockSpec((1, tile, d), lambda b, i: (b, i, 0)),
        out_shape=jax.ShapeDtypeStruct((bsz, seq, d), F32),
        scratch_shapes=[
            pltpu.VMEM((rows, d), BF16),
            pltpu.VMEM((2, 3, rows, cf), BF16),
            pltpu.VMEM((2, 3, rows, cf), BF16),
            pltpu.VMEM((tile, 2 * cf), BF16),
            pltpu.VMEM((tile, 2 * cf), BF16),
        ],
        compiler_params=pltpu.CompilerParams(
            dimension_semantics=("arbitrary", "arbitrary"), vmem_limit_bytes=VMEM_LIMIT_BYTES),
        name="conv_ffn_final" if final else "conv_ffn",
    )(h, h, h, modrows, *weights)


def _conformer_kernel(hp_ref, hc_ref, hn_ref, mod_ref, ng_ref, w1a_ref, w1g_ref, b1a_ref, b1g_ref, cw_ref, cb_ref,
                      lng_ref, lnb_ref, w2_ref, b2_ref, out_ref, xm_buf, v_buf, y_buf, s_buf, *, tile, n_tiles):
    ti = pl.program_id(1)
    shift, scale, gate = mod_ref[0, 0:1, :], mod_ref[0, 1:2, :], mod_ref[0, 2:3, :]
    gs = ng_ref[...] * (1.0 + scale)
    H = CF_HALO
    d = v_buf.shape[1]

    xm_buf[0:H, :] = _rms_mod(hp_ref[0], gs, shift).astype(BF16)
    xm_buf[H:H + tile, :] = _rms_mod(hc_ref[0], gs, shift).astype(BF16)
    xm_buf[H + tile:, :] = _rms_mod(hn_ref[0], gs, shift).astype(BF16)

    first = H - CF_CONV_LEFT
    n_taps = cw_ref.shape[0]
    RB = CF_ROW_BLOCK
    last_aligned = (first + n_taps - 1) // SUBLANES * SUBLANES
    assert first >= 0 and last_aligned + SUBLANES <= 2 * H and tile % CF_PW_SLAB == 0 and CF_PW_SLAB % RB == 0
    n_rows = tile + 2 * H

    def glu_slab(lo, hi):
        xm = xm_buf[lo:hi, :]
        a = jnp.dot(xm, w1a_ref[...], preferred_element_type=F32) + b1a_ref[...]
        g = jnp.dot(xm, w1g_ref[...], preferred_element_type=F32) + b1g_ref[...]
        v_buf[lo:hi, :] = a * _sigmoid(g)
        zeros = jnp.zeros((H, d), F32)
        if lo == 0:
            v_buf[0:H, :] = jnp.where(ti > 0, v_buf[0:H, :], zeros)
        if hi == n_rows:
            v_buf[H + tile:, :] = jnp.where(ti < n_tiles - 1, v_buf[H + tile:, :], zeros)

    def row_block(r0):
        for lb in range(d // MXU_DIM):
            lanes = slice(lb * MXU_DIM, (lb + 1) * MXU_DIM)
            y = None
            for res in range(SUBLANES):
                z = None
                for k in range(n_taps):
                    off = first + k
                    if off % SUBLANES != res:
                        continue
                    lo = r0 + off - res
                    term = cw_ref[k:k + 1, lanes] * v_buf[lo:lo + RB + SUBLANES, lanes]
                    z = term if z is None else z + term
                if z is not None:
                    zs = z[:RB, :] if res == 0 else pltpu.roll(z, RB + SUBLANES - res, 0)[:RB, :]
                    y = zs if y is None else y + zs
            y_buf[r0:r0 + RB, lanes] = y + cb_ref[:, lanes]

        yb = y_buf[r0:r0 + RB, :]
        mu = jnp.mean(yb, axis=-1, keepdims=True)
        yc = yb - mu
        var = jnp.mean(yc * yc, axis=-1, keepdims=True)
        yn = yc * lax.rsqrt(var + EPS) * lng_ref[...] + lnb_ref[...]
        s_buf[r0:r0 + RB, :] = (yn * _sigmoid(yn)).astype(BF16)

    def out_slab(lo, hi):
        o = jnp.dot(s_buf[lo:hi, :], w2_ref[...], preferred_element_type=F32) + b2_ref[...]
        out_ref[0, lo:hi, :] = hc_ref[0, lo:hi, :] + gate * o

    slab_ends = list(range(CF_PW_SLAB + H, n_rows - CF_PW_SLAB, CF_PW_SLAB)) + [n_rows]
    lo, next_block, next_out = 0, 0, 0
    for hi in slab_ends:
        glu_slab(lo, hi)
        lo = hi
        while next_block < tile and next_block + RB + 2 * H <= hi:
            row_block(next_block)
            next_block += RB
            if next_block - next_out == CF_PW_SLAB:
                out_slab(next_out, next_block)
                next_out = next_block


def _conformer_call(h, modrows, ng, w1, b1, cw, cb, lng, lnb, w2, b2, *, tile):
    bsz, seq, d = h.shape
    n_tiles = seq // tile
    hpt = tile // CF_HALO
    n_hblk = seq // CF_HALO
    weights = [ng, w1[:, :d].astype(BF16), w1[:, d:].astype(BF16), b1[None, :d], b1[None, d:], cw, cb[None],
               lng[None], lnb[None], w2.astype(BF16), b2[None]]
    in_specs = [
        pl.BlockSpec((1, CF_HALO, d), lambda b, i: (b, jnp.maximum(i * hpt - 1, 0), 0)),
        pl.BlockSpec((1, tile, d), lambda b, i: (b, i, 0)),
        pl.BlockSpec((1, CF_HALO, d), lambda b, i: (b, jnp.minimum((i + 1) * hpt, n_hblk - 1), 0)),
        pl.BlockSpec((1,) + modrows.shape[1:], lambda b, i: (b, 0, 0)),
    ] + [_const_spec(w.shape) for w in weights]
    kern = functools.partial(_conformer_kernel, tile=tile, n_tiles=n_tiles)
    return pl.pallas_call(
        kern,
        grid=(bsz, n_tiles),
        in_specs=in_specs,
        out_specs=pl.BlockSpec((1, tile, d), lambda b, i: (b, i, 0)),
        out_shape=jax.ShapeDtypeStruct((bsz, seq, d), F32),
        scratch_shapes=[
            pltpu.VMEM((tile + 2 * CF_HALO, d), BF16),
            pltpu.VMEM((tile + 2 * CF_HALO, d), F32),
            pltpu.VMEM((tile, d), F32),
            pltpu.VMEM((tile, d), BF16),
        ],
        compiler_params=pltpu.CompilerParams(
            dimension_semantics=("arbitrary", "arbitrary"), vmem_limit_bytes=VMEM_LIMIT_BYTES),
        name="conformer_conv",
    )(h, h, h, modrows, *weights)


def _tiles(seq):
    return dict(
        rglru=min(512, seq),
        ffn=min(1024, seq),
        conformer=min(1024, seq),
    )


def kernel(x, c, ctx, c_ctx, ada_w, ada_b, norm_mix, norm_ffn, rg_w_in, rg_conv_w, rg_conv_b, rg_wa, rg_ba, rg_wx, rg_bx, rg_lam, rg_w_out, cf_w_pw1, cf_b_pw1, cf_conv_w, cf_conv_b, cf_ln_g, cf_ln_b, cf_w_pw2, cf_b_pw2, ffn_w_up, ffn_conv_w, ffn_conv_b, ffn_w_down, norm_final):
    bsz, seq, d = x.shape
    depth = ada_w.shape[0]
    assert depth == 2, "layer 0 = RG-LRU, layer 1 = Conformer conv; the context stream feeds layer 0 only"
    assert seq % GRID_W == 0
    d_rnn = rg_w_in.shape[2] // 2
    head_dim = rg_wa.shape[3]
    tiles = _tiles(seq)
    ctx_len = ctx.shape[1]

    n_mod_rows = -(-(bsz + 1) // SUBLANES) * SUBLANES
    cc = jnp.concatenate([c, c_ctx[None], jnp.zeros((n_mod_rows - bsz - 1, d), F32)], axis=0)
    mod = _modulation(cc, ada_w, ada_b)

    def modrows(layer, rows):
        m = mod[layer][rows].reshape(-1, 6, d)
        return jnp.concatenate([m, jnp.zeros((m.shape[0], SUBLANES - 6, d), F32)], axis=1)

    lat_rows = jnp.arange(bsz)
    ctx_rows = jnp.full((bsz,), bsz)

    k = 0
    gate_starts = _gate_windows(d_rnn, head_dim)
    ng = norm_mix[0][None]
    winx = rg_w_in[k][:, d_rnn:].astype(BF16)
    wing = rg_w_in[k][:, :d_rnn].astype(BF16)
    cw, cb = rg_conv_w[k], rg_conv_b[k][None]
    lam = rg_lam[k]
    zero_state = jnp.zeros((bsz, 1, d_rnn), F32)
    per_dir = []
    for z in range(2):
        per_dir.append(dict(
            wg=_pack_gate_weights(rg_wa[k, z], rg_wx[k, z], gate_starts),
            ba=rg_ba[k, z].reshape(1, d_rnn), bx=rg_bx[k, z].reshape(1, d_rnn), lam=lam[z][None]))

    proj = (ng, winx, cw, cb)

    def rg(inp, mrows, z, h0, mode, tile, **kw):
        p = per_dir[z]
        return _rglru_call(inp, mrows, (p["wg"], p["ba"], p["bx"], p["lam"]), h0,
                           tile=tile, reverse=bool(z), mode=mode, gate_starts=gate_starts, **kw)

    ctx_mod = modrows(0, ctx_rows)
    lat_mod0 = modrows(0, lat_rows)
    hf0 = rg(ctx, ctx_mod, 0, zero_state, "state", ctx_len, proj=proj)
    hb0 = rg(ctx, ctx_mod, 1, zero_state, "state", ctx_len, proj=proj)
    hb, uh, ug = rg(x, lat_mod0, 1, hb0, "store", tiles["rglru"], proj=proj, wing=wing)
    h = rg(x, lat_mod0, 0, hf0, "out", tiles["rglru"], stored=(uh, ug, hb), wout=rg_w_out[k].astype(BF16))

    h = _ffn_call(h, lat_mod0, norm_ffn[0][None], ffn_w_up[0], ffn_conv_w[0], ffn_conv_b[0], ffn_w_down[0],
                  tile=tiles["ffn"], width=GRID_W)

    lat_mod1 = modrows(1, lat_rows)
    h = _conformer_call(h, lat_mod1, norm_mix[1][None], cf_w_pw1[0], cf_b_pw1[0], cf_conv_w[0], cf_conv_b[0],
                        cf_ln_g[0], cf_ln_b[0], cf_w_pw2[0], cf_b_pw2[0], tile=tiles["conformer"])
    h = _ffn_call(h, lat_mod1, norm_ffn[1][None], ffn_w_up[1], ffn_conv_w[1], ffn_conv_b[1], ffn_w_down[1],
                  tile=tiles["ffn"], width=GRID_W, norm_final=norm_final[None])
    return h
```

```python
import functools

import jax
import jax.numpy as jnp
from jax import lax
from jax.experimental import pallas as pl
from jax.experimental.pallas import tpu as pltpu

F32 = jnp.float32
BF16 = jnp.bfloat16

EPS = 1e-6
RG_C = 8.0
GRID_W = 64
RG_CONV_LEFT = 2
CF_CONV_LEFT = 15

SUBLANES = 8
LANES = 128
MXU_DIM = 256
VMEM_LIMIT_BYTES = 56 * 1024 * 1024

RG_HALO = SUBLANES
CF_HALO = 2 * SUBLANES
GATE_K = 2 * MXU_DIM
CF_ROW_BLOCK = 64
CF_PW_SLAB = 256
BF16_ROWS = 2 * SUBLANES
FFN_SLAB_GRID_ROWS = 4


def _const_spec(shape):
    nd = len(shape)
    return pl.BlockSpec(shape, lambda *_: (0,) * nd, pipeline_mode=pl.Buffered(1))


def _sigmoid(x):
    return 0.5 * jnp.tanh(0.5 * x) + 0.5


def _rms_mod(x, gs, shift):
    ms = jnp.mean(x * x, axis=-1, keepdims=True)
    return x * lax.rsqrt(ms + EPS) * gs + shift


def _mod_kernel(cc_ref, w_ref, b_ref, o_ref):
    s = cc_ref[...]
    s = s * jax.nn.sigmoid(s)
    o_ref[0] = jnp.dot(s, w_ref[0], preferred_element_type=F32) + b_ref[0]


def _modulation(cc, ada_w, ada_b):
    depth, d, n = ada_w.shape
    rows = cc.shape[0]
    nc = n // 6
    return pl.pallas_call(
        _mod_kernel,
        grid=(depth, n // nc),
        in_specs=[
            pl.BlockSpec((rows, d), lambda i, j: (0, 0)),
            pl.BlockSpec((1, d, nc), lambda i, j: (i, 0, j)),
            pl.BlockSpec((1, 1, nc), lambda i, j: (i, 0, j)),
        ],
        out_specs=pl.BlockSpec((1, rows, nc), lambda i, j: (i, 0, j)),
        out_shape=jax.ShapeDtypeStruct((depth, rows, n), F32),
        compiler_params=pltpu.CompilerParams(
            dimension_semantics=("arbitrary", "arbitrary"), vmem_limit_bytes=VMEM_LIMIT_BYTES),
        name="adaln_mod",
    )(cc, ada_w, ada_b.reshape(depth, 1, n))


def _gate_windows(d_rnn, head_dim):
    starts = []
    for j in range(d_rnn // MXU_DIM):
        first_row = (j * MXU_DIM // head_dim) * head_dim
        last_row = ((j * MXU_DIM + MXU_DIM - 1) // head_dim + 1) * head_dim
        k0 = min(first_row // LANES * LANES, d_rnn - GATE_K)
        assert k0 <= first_row and last_row <= k0 + GATE_K
        starts.append(k0)
    return tuple(starts)


def _pack_gate_weights(wa, wx, starts):
    dense_a = jax.scipy.linalg.block_diag(*[wa[h] for h in range(wa.shape[0])])
    dense_x = jax.scipy.linalg.block_diag(*[wx[h] for h in range(wx.shape[0])])
    groups = []
    for j, k0 in enumerate(starts):
        cols = slice(j * MXU_DIM, (j + 1) * MXU_DIM)
        groups.append(jnp.concatenate([dense_a[k0:k0 + GATE_K, cols], dense_x[k0:k0 + GATE_K, cols]], axis=1))
    return jnp.stack(groups).astype(BF16)


def _scan_pitch(seg):
    pitch = seg
    while (pitch // SUBLANES) % 2 == 0:
        pitch += SUBLANES
    return pitch


def _rglru_kernel(*refs, tile, n_tiles, reverse, mode, halo, gate_starts):
    refs = list(refs)

    def take(n):
        head = refs[:n]
        del refs[:n]
        return head

    from_x = mode != "out"
    if from_x:
        xp_ref, xc_ref, xn_ref = take(3) if halo else (None, take(1)[0], None)
        mod_ref, ng_ref, winx_ref, cw_ref, cb_ref = take(5)
    else:
        xc_ref, mod_ref, uh_in_ref, ug_in_ref, hb_ref = take(5)
    wg_ref, ba_ref, bx_ref, lam_ref, h0_ref = take(5)
    if mode == "store":
        (wing_ref,) = take(1)
        out_ref, uh_out_ref, ug_out_ref = take(3)
    else:
        if mode == "out":
            (wout_ref,) = take(1)
        (out_ref,) = take(1)
    if from_x:
        xm_buf, ux_buf, uh_buf = take(3)
    a_buf, b_buf, hl_buf, pr_buf, carry_ref = take(5)
    if mode == "out":
        (v_buf,) = take(1)

    step = pl.program_id(1)
    ti = (n_tiles - 1 - step) if reverse else step
    H = RG_HALO
    d_rnn = lam_ref.shape[1]
    n_lane_blocks = d_rnn // LANES
    seg = tile // SUBLANES
    pitch = a_buf.shape[1] // SUBLANES

    if from_x:
        shift, scale = mod_ref[0, 0:1, :], mod_ref[0, 1:2, :]
        gs = ng_ref[...] * (1.0 + scale)
        xm_buf[H:H + tile, :] = _rms_mod(xc_ref[0], gs, shift)
        zero_halo = jnp.zeros((H, xm_buf.shape[1]), F32)
        if halo:
            xm_buf[0:H, :] = jnp.where(ti > 0, _rms_mod(xp_ref[0], gs, shift), zero_halo)
            xm_buf[H + tile:, :] = jnp.where(ti < n_tiles - 1, _rms_mod(xn_ref[0], gs, shift), zero_halo)
        else:
            xm_buf[0:H, :] = zero_halo
            xm_buf[H + tile:, :] = zero_halo

        ux_buf[...] = jnp.dot(xm_buf[...].astype(BF16), winx_ref[...], preferred_element_type=F32)

        n_buf_rows = tile + 2 * H
        for j in range(d_rnn // MXU_DIM):
            cols = slice(j * MXU_DIM, (j + 1) * MXU_DIM)
            ux = ux_buf[:, cols]
            uh = cb_ref[:, cols]
            for k in range(cw_ref.shape[0]):
                off = H - RG_CONV_LEFT + k
                tap = ux[off:off + tile] if off % SUBLANES == 0 else pltpu.roll(ux, n_buf_rows - off, 0)[:tile]
                uh = uh + cw_ref[k:k + 1, cols] * tap
            uh_buf[:, cols] = uh
        uh_src = uh_buf
    else:
        uh_src = uh_in_ref.at[0]

    neg_c_softplus = (-RG_C) * jax.nn.softplus(-lam_ref[...])
    for j, k0 in enumerate(gate_starts):
        cols = slice(j * MXU_DIM, (j + 1) * MXU_DIM)
        pre = jnp.dot(uh_src[:, k0:k0 + GATE_K].astype(BF16), wg_ref[j], preferred_element_type=F32)
        r = _sigmoid(pre[:, :MXU_DIM] + ba_ref[:, cols])
        ig = _sigmoid(pre[:, MXU_DIM:] + bx_ref[:, cols])
        log_a = r * neg_c_softplus[:, cols]
        a = jnp.exp(log_a)
        m2 = jnp.tanh(-log_a) * (1.0 + a * a)
        m = jnp.where(m2 > 0.0, m2 * lax.rsqrt(m2), 0.0)
        b = m * (ig * uh_src[:, cols])
        for half in range(MXU_DIM // LANES):
            lanes = slice(half * LANES, (half + 1) * LANES)
            jb = j * (MXU_DIM // LANES) + half
            for s in range(SUBLANES):
                a_buf[jb, s * pitch:s * pitch + seg, :] = a[s * seg:(s + 1) * seg, lanes]
                b_buf[jb, s * pitch:s * pitch + seg, :] = b[s * seg:(s + 1) * seg, lanes]

    @pl.when(step == 0)
    def _():
        carry_ref[...] = jnp.broadcast_to(h0_ref[0], carry_ref.shape)

    row = lax.broadcasted_iota(jnp.int32, (SUBLANES, LANES), 0)

    def local_step(i, carry):
        q = (seg - 1 - i) if reverse else i
        r0 = q * SUBLANES
        hs, prods = [], []
        for jb in range(n_lane_blocks):
            a = a_buf[jb, pl.ds(q, SUBLANES, stride=pitch), :]
            b = b_buf[jb, pl.ds(q, SUBLANES, stride=pitch), :]
            h = a * carry[0][jb] + b
            p = a * carry[1][jb]
            hl_buf[jb, pl.ds(r0, SUBLANES), :] = h
            pr_buf[jb, pl.ds(r0, SUBLANES), :] = p
            hs.append(h)
            prods.append(p)
        return tuple(hs), tuple(prods)

    zeros = tuple(jnp.zeros((SUBLANES, LANES), F32) for _ in range(n_lane_blocks))
    ones = tuple(jnp.ones((SUBLANES, LANES), F32) for _ in range(n_lane_blocks))
    carry = (zeros, ones)
    for i in range(seg):
        carry = local_step(i, carry)
    h_end, p_end = carry

    seg_in = []
    for jb in range(n_lane_blocks):
        lanes = slice(jb * LANES, (jb + 1) * LANES)
        a, b = p_end[jb], h_end[jb]
        for d in (1, 2, 4):
            if reverse:
                keep = row < SUBLANES - d
                sh = SUBLANES - d
            else:
                keep = row >= d
                sh = d
            a_s = jnp.where(keep, pltpu.roll(a, sh, 0), 1.0)
            b_s = jnp.where(keep, pltpu.roll(b, sh, 0), 0.0)
            b = a * b_s + b
            a = a * a_s
        carry_in = carry_ref[:, lanes]
        seg_out = b + a * carry_in
        if reverse:
            seg_in.append(jnp.where(row < SUBLANES - 1, pltpu.roll(seg_out, SUBLANES - 1, 0), carry_in))
            edge = seg_out[0:1, :]
        else:
            seg_in.append(jnp.where(row >= 1, pltpu.roll(seg_out, 1, 0), carry_in))
            edge = seg_out[SUBLANES - 1:SUBLANES, :]
        carry_ref[:, lanes] = jnp.broadcast_to(edge, (SUBLANES, LANES))

    def fix_step(q, carry):
        r0 = q * SUBLANES
        for jb in range(n_lane_blocks):
            h = hl_buf[jb, pl.ds(r0, SUBLANES), :] + pr_buf[jb, pl.ds(r0, SUBLANES), :] * seg_in[jb]
            b_buf[jb, pl.ds(q, SUBLANES, stride=pitch), :] = h
        return carry

    for q in range(seg):
        fix_step(q, 0)

    def states(jb):
        return jnp.concatenate([b_buf[jb, s * pitch:s * pitch + seg, :] for s in range(SUBLANES)], axis=0)

    if mode == "state":
        out_ref[0] = carry_ref[0:1, :]
    elif mode == "store":
        for jb in range(n_lane_blocks):
            out_ref[0, :, jb * LANES:(jb + 1) * LANES] = states(jb)
        uh_out_ref[0] = uh_buf[...]
        ug_out_ref[0] = jnp.dot(xm_buf[H:H + tile, :].astype(BF16), wing_ref[...], preferred_element_type=F32)
    else:
        for jb in range(n_lane_blocks):
            lanes = slice(jb * LANES, (jb + 1) * LANES)
            v_buf[:, lanes] = ((states(jb) + hb_ref[0, :, lanes])
                               * jax.nn.gelu(ug_in_ref[0, :, lanes])).astype(BF16)
        y = jnp.dot(v_buf[...], wout_ref[...], preferred_element_type=F32)
        out_ref[0] = xc_ref[0] + mod_ref[0, 2:3, :] * y


def _rglru_call(x, modrows, gates, h0, *, tile, reverse, mode, gate_starts, proj=None, stored=None, wing=None,
                wout=None):
    bsz, seq, d = x.shape
    d_rnn = gates[3].shape[1]
    n_tiles = seq // tile
    halo = n_tiles > 1
    hpt = tile // RG_HALO
    n_hblk = seq // RG_HALO
    n_lane_blocks = d_rnn // LANES
    assert tile % (SUBLANES * SUBLANES) == 0
    pitch = _scan_pitch(tile // SUBLANES)
    from_x = mode != "out"

    def t_of(i):
        return (n_tiles - 1 - i) if reverse else i

    tile_spec_d = pl.BlockSpec((1, tile, d), lambda b, i: (b, t_of(i), 0))
    tile_spec_r = pl.BlockSpec((1, tile, d_rnn), lambda b, i: (b, t_of(i), 0))
    mod_spec = pl.BlockSpec((1,) + modrows.shape[1:], lambda b, i: (b, 0, 0))

    in_specs, args = [], []
    if from_x:
        if halo:
            in_specs.append(pl.BlockSpec((1, RG_HALO, d), lambda b, i: (b, jnp.maximum(t_of(i) * hpt - 1, 0), 0)))
            args.append(x)
        in_specs.append(tile_spec_d)
        args.append(x)
        if halo:
            in_specs.append(
                pl.BlockSpec((1, RG_HALO, d), lambda b, i: (b, jnp.minimum((t_of(i) + 1) * hpt, n_hblk - 1), 0)))
            args.append(x)
        in_specs.append(mod_spec)
        args.append(modrows)
        consts = list(proj) + list(gates)
    else:
        in_specs += [tile_spec_d, mod_spec, tile_spec_r, tile_spec_r, tile_spec_r]
        args += [x, modrows, *stored]
        consts = list(gates)
    for w in consts:
        in_specs.append(_const_spec(w.shape))
        args.append(w)
    in_specs.append(pl.BlockSpec((1, 1, d_rnn), lambda b, i: (b, 0, 0)))
    args.append(h0)
    for w in ([wing] if mode == "store" else [wout] if mode == "out" else []):
        in_specs.append(_const_spec(w.shape))
        args.append(w)

    if mode == "state":
        out_specs = pl.BlockSpec((1, 1, d_rnn), lambda b, i: (b, 0, 0))
        out_shape = jax.ShapeDtypeStruct((bsz, 1, d_rnn), F32)
    elif mode == "store":
        out_specs = [tile_spec_r] * 3
        out_shape = [jax.ShapeDtypeStruct((bsz, seq, d_rnn), F32)] * 3
    else:
        out_specs = tile_spec_d
        out_shape = jax.ShapeDtypeStruct((bsz, seq, d), F32)

    scratch = []
    if from_x:
        scratch += [
            pltpu.VMEM((tile + 2 * RG_HALO, d), F32),
            pltpu.VMEM((tile + 2 * RG_HALO, d_rnn), F32),
            pltpu.VMEM((tile, d_rnn), F32),
        ]
    scratch += [
        pltpu.VMEM((n_lane_blocks, SUBLANES * pitch, LANES), F32),
        pltpu.VMEM((n_lane_blocks, SUBLANES * pitch, LANES), F32),
        pltpu.VMEM((n_lane_blocks, tile, LANES), F32),
        pltpu.VMEM((n_lane_blocks, tile, LANES), F32),
        pltpu.VMEM((SUBLANES, d_rnn), F32),
    ]
    if mode == "out":
        scratch.append(pltpu.VMEM((tile, d_rnn), BF16))

    kern = functools.partial(_rglru_kernel, tile=tile, n_tiles=n_tiles, reverse=reverse, mode=mode, halo=halo,
                             gate_starts=gate_starts)
    return pl.pallas_call(
        kern,
        grid=(bsz, n_tiles),
        in_specs=in_specs,
        out_specs=out_specs,
        out_shape=out_shape,
        scratch_shapes=scratch,
        compiler_params=pltpu.CompilerParams(
            dimension_semantics=("arbitrary", "arbitrary"), vmem_limit_bytes=VMEM_LIMIT_BYTES),
        name=f"rglru_{mode}_{'bwd' if reverse else 'fwd'}",
    )(*args)


def _shift_rows_down(x):
    r = pltpu.roll(x, 1, 0)
    sub = lax.broadcasted_iota(jnp.int32, (SUBLANES, x.shape[1]), 0)
    return jnp.concatenate([jnp.where(sub == 0, 0.0, r[:SUBLANES]), r[SUBLANES:]], axis=0)


def _shift_rows_up(x):
    n = x.shape[0]
    r = pltpu.roll(x, n - 1, 0)
    sub = lax.broadcasted_iota(jnp.int32, (SUBLANES, x.shape[1]), 0)
    return jnp.concatenate([r[:n - SUBLANES], jnp.where(sub == SUBLANES - 1, 0.0, r[n - SUBLANES:])], axis=0)


def _ffn_kernel(*refs, tile, n_tiles, width, final):
    refs = list(refs)
    hp_ref, hc_ref, hn_ref, mod_ref, ng_ref, wu_ref, cwv_ref, cwg_ref, cbv_ref, cbg_ref, wd_ref = refs[:11]
    refs = refs[11:]
    if final:
        nf_ref = refs[0]
        refs = refs[1:]
    out_ref, xm_buf, u_a, u_b, gated_a, gated_b = refs
    u_slots, gated_slots = (u_a, u_b), (gated_a, gated_b)

    ti = pl.program_id(1)
    shift, scale, gate = mod_ref[0, 3:4, :], mod_ref[0, 4:5, :], mod_ref[0, 5:6, :]
    gs = ng_ref[...] * (1.0 + scale)
    W = width
    n_chunks, cf, _ = wd_ref.shape
    n_grid_rows = tile // W
    slab = FFN_SLAB_GRID_ROWS * W
    assert n_chunks >= 3 and tile % slab == 0 and 2 * W <= slab
    top0, bot0 = tile, tile + W

    zero_halo = jnp.zeros((W, xm_buf.shape[1]), F32)
    xm_buf[0:tile, :] = _rms_mod(hc_ref[0], gs, shift).astype(BF16)
    xm_buf[top0:top0 + W, :] = jnp.where(ti > 0, _rms_mod(hp_ref[0], gs, shift), zero_halo).astype(BF16)
    xm_buf[bot0:bot0 + W, :] = jnp.where(ti < n_tiles - 1, _rms_mod(hn_ref[0], gs, shift), zero_halo).astype(BF16)

    def grid_row_start(r):
        return top0 if r < 0 else (bot0 if r >= n_grid_rows else r * W)

    def up_slab(c, slot, j):
        start = j * slab
        n_rows = min(slab, xm_buf.shape[0] - start)
        y = jnp.dot(xm_buf[start:start + n_rows, :], wu_ref[c], preferred_element_type=F32)
        for which in range(2):
            for k in range(n_rows // W):
                blk = y[k * W:(k + 1) * W, which * cf:(which + 1) * cf]
                r0 = start + k * W
                u_slots[slot][which, 0, r0:r0 + W, :] = _shift_rows_down(blk).astype(BF16)
                u_slots[slot][which, 1, r0:r0 + W, :] = blk.astype(BF16)
                u_slots[slot][which, 2, r0:r0 + W, :] = _shift_rows_up(blk).astype(BF16)

    def conv_rows(u_ref, which, cw, cb, r):
        acc = cb
        for dr in range(3):
            r0 = grid_row_start(r + dr - 1)
            for dc in range(3):
                acc = acc + cw[3 * dr + dc:3 * dr + dc + 1, :] * u_ref[which, dc, r0:r0 + W, :]
        return acc

    def conv_slab(c, i):
        cwv, cwg = cwv_ref[c].astype(BF16), cwg_ref[c].astype(BF16)
        cbv, cbg = cbv_ref[c].astype(BF16), cbg_ref[c].astype(BF16)
        u_ref, gated_ref, half = u_slots[c % 2], gated_slots[(c // 2) % 2], c % 2
        for r in range(i * FFN_SLAB_GRID_ROWS, (i + 1) * FFN_SLAB_GRID_ROWS):
            v = conv_rows(u_ref, 0, cwv, cbv, r)
            g = conv_rows(u_ref, 1, cwg, cbg, r)
            gated_ref[r * W:(r + 1) * W, half * cf:(half + 1) * cf] = g * jax.nn.sigmoid(g) * v

    def down_slab(c0, n, i):
        sl = slice(i * slab, (i + 1) * slab)
        w = wd_ref[c0:c0 + n].reshape(n * cf, wd_ref.shape[2])
        y = jnp.dot(gated_slots[(c0 // 2) % 2][sl, :n * cf], w, preferred_element_type=F32)
        if c0 == 0:
            out_ref[0, sl, :] = y
        else:
            out_ref[0, sl, :] += y

    n_conv_slabs = tile // slab
    n_up_slabs = n_conv_slabs + 1

    for s in range(n_chunks + 2):
        c_up = s if s < n_chunks else None
        c_conv = s - 1 if 1 <= s <= n_chunks else None
        c_down = None
        if s >= 3 and (s - 3) % 2 == 0 and s - 2 < n_chunks:
            c_down = (s - 3, 2)
        elif s == n_chunks + 1 and n_chunks % 2 == 1:
            c_down = (n_chunks - 1, 1)
        for i in range(n_up_slabs):
            if c_down is not None and i < n_conv_slabs:
                down_slab(*c_down, i)
            if c_up is not None:
                up_slab(c_up, c_up % 2, i)
            if c_conv is not None and i < n_conv_slabs:
                conv_slab(c_conv, i)

    h = hc_ref[0] + gate * out_ref[0]
    if final:
        ms = jnp.mean(h * h, axis=-1, keepdims=True)
        h = h * lax.rsqrt(ms + EPS) * nf_ref[...]
    out_ref[0] = h


def _ffn_call(h, modrows, ng, w_up, conv_w, conv_b, w_down, *, tile, width, norm_final=None):
    bsz, seq, d = h.shape
    d_ff = w_down.shape[0]
    cf = MXU_DIM
    n_chunks = d_ff // cf
    n_tiles = seq // tile
    rpt = tile // width
    n_rows = seq // width

    def chunked_cols(w):
        return w.reshape(w.shape[0], n_chunks, cf).transpose(1, 0, 2)

    cw = conv_w.reshape(9, 2 * d_ff)
    w_up_b = w_up.astype(BF16)
    weights = [
        ng,
        jnp.concatenate([chunked_cols(w_up_b[:, :d_ff]), chunked_cols(w_up_b[:, d_ff:])], axis=2),
        chunked_cols(cw[:, :d_ff]), chunked_cols(cw[:, d_ff:]),
        chunked_cols(conv_b[None, :d_ff]), chunked_cols(conv_b[None, d_ff:]),
        w_down.reshape(n_chunks, cf, d).astype(BF16),
    ]
    final = norm_final is not None
    if final:
        weights.append(norm_final)

    in_specs = [
        pl.BlockSpec((1, width, d), lambda b, i: (b, jnp.maximum(i * rpt - 1, 0), 0)),
        pl.BlockSpec((1, tile, d), lambda b, i: (b, i, 0)),
        pl.BlockSpec((1, width, d), lambda b, i: (b, jnp.minimum((i + 1) * rpt, n_rows - 1), 0)),
        pl.BlockSpec((1,) + modrows.shape[1:], lambda b, i: (b, 0, 0)),
    ] + [_const_spec(w.shape) for w in weights]

    rows = tile + 2 * width
    kern = functools.partial(_ffn_kernel, tile=tile, n_tiles=n_tiles, width=width, final=final)
    return pl.pallas_call(
        kern,
        grid=(bsz, n_tiles),
        in_specs=in_specs,
        out_specs=pl.BlockSpec((1, tile, d), lambda b, i: (b, i, 0)),
        out_shape=jax.ShapeDtypeStruct((bsz, seq, d), F32),
        scratch_shapes=[
            pltpu.VMEM((rows, d), BF16),
            pltpu.VMEM((2, 3, rows, cf), BF16),
            pltpu.VMEM((2, 3, rows, cf), BF16),
            pltpu.VMEM((tile, 2 * cf), BF16),
            pltpu.VMEM((tile, 2 * cf), BF16),
        ],
        compiler_params=pltpu.CompilerParams(
            dimension_semantics=("arbitrary", "arbitrary"), vmem_limit_bytes=VMEM_LIMIT_BYTES),
        name="conv_ffn_final" if final else "conv_ffn",
    )(h, h, h, modrows, *weights)


def _conformer_kernel(hp_ref, hc_ref, hn_ref, mod_ref, ng_ref, w1a_ref, w1g_ref, b1a_ref, b1g_ref, cw_ref, cb_ref,
                      lng_ref, lnb_ref, w2_ref, b2_ref, out_ref, xm_buf, v_buf, y_buf, s_buf, *, tile, n_tiles):
    ti = pl.program_id(1)
    shift, scale, gate = mod_ref[0, 0:1, :], mod_ref[0, 1:2, :], mod_ref[0, 2:3, :]
    gs = ng_ref[...] * (1.0 + scale)
    H = CF_HALO
    d = v_buf.shape[1]

    xm_buf[0:H, :] = _rms_mod(hp_ref[0], gs, shift).astype(BF16)
    xm_buf[H:H + tile, :] = _rms_mod(hc_ref[0], gs, shift).astype(BF16)
    xm_buf[H + tile:, :] = _rms_mod(hn_ref[0], gs, shift).astype(BF16)

    first = H - CF_CONV_LEFT
    n_taps = cw_ref.shape[0]
    RB = CF_ROW_BLOCK
    last_aligned = (first + n_taps - 1) // SUBLANES * SUBLANES
    assert first >= 0 and last_aligned + SUBLANES <= 2 * H and tile % CF_PW_SLAB == 0 and CF_PW_SLAB % RB == 0
    n_rows = tile + 2 * H

    def glu_slab(lo, hi):
        xm = xm_buf[lo:hi, :]
        a = jnp.dot(xm, w1a_ref[...], preferred_element_type=F32) + b1a_ref[...]
        g = jnp.dot(xm, w1g_ref[...], preferred_element_type=F32) + b1g_ref[...]
        v_buf[lo:hi, :] = a * _sigmoid(g)
        zeros = jnp.zeros((H, d), F32)
        if lo == 0:
            v_buf[0:H, :] = jnp.where(ti > 0, v_buf[0:H, :], zeros)
        if hi == n_rows:
            v_buf[H + tile:, :] = jnp.where(ti < n_tiles - 1, v_buf[H + tile:, :], zeros)

    def row_block(r0):
        for lb in range(d // MXU_DIM):
            lanes = slice(lb * MXU_DIM, (lb + 1) * MXU_DIM)
            y = None
            for res in range(SUBLANES):
                z = None
                for k in range(n_taps):
                    off = first + k
                    if off % SUBLANES != res:
                        continue
                    lo = r0 + off - res
                    term = cw_ref[k:k + 1, lanes] * v_buf[lo:lo + RB + SUBLANES, lanes]
                    z = term if z is None else z + term
                if z is not None:
                    zs = z[:RB, :] if res == 0 else pltpu.roll(z, RB + SUBLANES - res, 0)[:RB, :]
                    y = zs if y is None else y + zs
            y_buf[r0:r0 + RB, lanes] = y + cb_ref[:, lanes]

        yb = y_buf[r0:r0 + RB, :]
        mu = jnp.mean(yb, axis=-1, keepdims=True)
        yc = yb - mu
        var = jnp.mean(yc * yc, axis=-1, keepdims=True)
        yn = yc * lax.rsqrt(var + EPS) * lng_ref[...] + lnb_ref[...]
        s_buf[r0:r0 + RB, :] = (yn * _sigmoid(yn)).astype(BF16)

    def out_slab(lo, hi):
        o = jnp.dot(s_buf[lo:hi, :], w2_ref[...], preferred_element_type=F32) + b2_ref[...]
        out_ref[0, lo:hi, :] = hc_ref[0, lo:hi, :] + gate * o

    slab_ends = list(range(CF_PW_SLAB + H, n_rows - CF_PW_SLAB, CF_PW_SLAB)) + [n_rows]
    lo, next_block, next_out = 0, 0, 0
    for hi in slab_ends:
        glu_slab(lo, hi)
        lo = hi
        while next_block < tile and next_block + RB + 2 * H <= hi:
            row_block(next_block)
            next_block += RB
            if next_block - next_out == CF_PW_SLAB:
                out_slab(next_out, next_block)
                next_out = next_block


def _conformer_call(h, modrows, ng, w1, b1, cw, cb, lng, lnb, w2, b2, *, tile):
    bsz, seq, d = h.shape
    n_tiles = seq // tile
    hpt = tile // CF_HALO
    n_hblk = seq // CF_HALO
    weights = [ng, w1[:, :d].astype(BF16), w1[:, d:].astype(BF16), b1[None, :d], b1[None, d:], cw, cb[None],
               lng[None], lnb[None], w2.astype(BF16), b2[None]]
    in_specs = [
        pl.BlockSpec((1, CF_HALO, d), lambda b, i: (b, jnp.maximum(i * hpt - 1, 0), 0)),
        pl.BlockSpec((1, tile, d), lambda b, i: (b, i, 0)),
        pl.BlockSpec((1, CF_HALO, d), lambda b, i: (b, jnp.minimum((i + 1) * hpt, n_hblk - 1), 0)),
        pl.BlockSpec((1,) + modrows.shape[1:], lambda b, i: (b, 0, 0)),
    ] + [_const_spec(w.shape) for w in weights]
    kern = functools.partial(_conformer_kernel, tile=tile, n_tiles=n_tiles)
    return pl.pallas_call(
        kern,
        grid=(bsz, n_tiles),
        in_specs=in_specs,
        out_specs=pl.BlockSpec((1, tile, d), lambda b, i: (b, i, 0)),
        out_shape=jax.ShapeDtypeStruct((bsz, seq, d), F32),
        scratch_shapes=[
            pltpu.VMEM((tile + 2 * CF_HALO, d), BF16),
            pltpu.VMEM((tile + 2 * CF_HALO, d), F32),
            pltpu.VMEM((tile, d), F32),
            pltpu.VMEM((tile, d), BF16),
        ],
        compiler_params=pltpu.CompilerParams(
            dimension_semantics=("arbitrary", "arbitrary"), vmem_limit_bytes=VMEM_LIMIT_BYTES),
        name="conformer_conv",
    )(h, h, h, modrows, *weights)


def _tiles(seq):
    return dict(
        rglru=min(512, seq),
        ffn=min(1024, seq),
        conformer=min(1024, seq),
    )


def kernel(x, c, ctx, c_ctx, ada_w, ada_b, norm_mix, norm_ffn, rg_w_in, rg_conv_w, rg_conv_b, rg_wa, rg_ba, rg_wx, rg_bx, rg_lam, rg_w_out, cf_w_pw1, cf_b_pw1, cf_conv_w, cf_conv_b, cf_ln_g, cf_ln_b, cf_w_pw2, cf_b_pw2, ffn_w_up, ffn_conv_w, ffn_conv_b, ffn_w_down, norm_final):
    bsz, seq, d = x.shape
    depth = ada_w.shape[0]
    assert depth == 2, "layer 0 = RG-LRU, layer 1 = Conformer conv; the context stream feeds layer 0 only"
    assert seq % GRID_W == 0
    d_rnn = rg_w_in.shape[2] // 2
    head_dim = rg_wa.shape[3]
    tiles = _tiles(seq)
    ctx_len = ctx.shape[1]

    n_mod_rows = -(-(bsz + 1) // SUBLANES) * SUBLANES
    cc = jnp.concatenate([c, c_ctx[None], jnp.zeros((n_mod_rows - bsz - 1, d), F32)], axis=0)
    mod = _modulation(cc, ada_w, ada_b)

    def modrows(layer, rows):
        m = mod[layer][rows].reshape(-1, 6, d)
        return jnp.concatenate([m, jnp.zeros((m.shape[0], SUBLANES - 6, d), F32)], axis=1)

    lat_rows = jnp.arange(bsz)
    ctx_rows = jnp.full((bsz,), bsz)

    k = 0
    gate_starts = _gate_windows(d_rnn, head_dim)
    ng = norm_mix[0][None]
    winx = rg_w_in[k][:, d_rnn:].astype(BF16)
    wing = rg_w_in[k][:, :d_rnn].astype(BF16)
    cw, cb = rg_conv_w[k], rg_conv_b[k][None]
    lam = rg_lam[k]
    zero_state = jnp.zeros((bsz, 1, d_rnn), F32)
    per_dir = []
    for z in range(2):
        per_dir.append(dict(
            wg=_pack_gate_weights(rg_wa[k, z], rg_wx[k, z], gate_starts),
            ba=rg_ba[k, z].reshape(1, d_rnn), bx=rg_bx[k, z].reshape(1, d_rnn), lam=lam[z][None]))

    proj = (ng, winx, cw, cb)

    def rg(inp, mrows, z, h0, mode, tile, **kw):
        p = per_dir[z]
        return _rglru_call(inp, mrows, (p["wg"], p["ba"], p["bx"], p["lam"]), h0,
                           tile=tile, reverse=bool(z), mode=mode, gate_starts=gate_starts, **kw)

    ctx_mod = modrows(0, ctx_rows)
    lat_mod0 = modrows(0, lat_rows)
    hf0 = rg(ctx, ctx_mod, 0, zero_state, "state", ctx_len, proj=proj)
    hb0 = rg(ctx, ctx_mod, 1, zero_state, "state", ctx_len, proj=proj)
    hb, uh, ug = rg(x, lat_mod0, 1, hb0, "store", tiles["rglru"], proj=proj, wing=wing)
    h = rg(x, lat_mod0, 0, hf0, "out", tiles["rglru"], stored=(uh, ug, hb), wout=rg_w_out[k].astype(BF16))

    h = _ffn_call(h, lat_mod0, norm_ffn[0][None], ffn_w_up[0], ffn_conv_w[0], ffn_conv_b[0], ffn_w_down[0],
                  tile=tiles["ffn"], width=GRID_W)

    lat_mod1 = modrows(1, lat_rows)
    h = _conformer_call(h, lat_mod1, norm_mix[1][None], cf_w_pw1[0], cf_b_pw1[0], cf_conv_w[0], cf_conv_b[0],
                        cf_ln_g[0], cf_ln_b[0], cf_w_pw2[0], cf_b_pw2[0], tile=tiles["conformer"])
    h = _ffn_call(h, lat_mod1, norm_ffn[1][None], ffn_w_up[1], ffn_conv_w[1], ffn_conv_b[1], ffn_w_down[1],
                  tile=tiles["ffn"], width=GRID_W, norm_final=norm_final[None])
    return h
```
